```python
import functools
import jax, jax.numpy as jnp
from jax import lax
import numpy as np

D_MODEL = 2048
BATCH = 1
SEQ = 8192
DEPTH = 1
DEC_BATCH = 8
DEC_SEQ = 16
PAST_LEN = 4096

CHUNK = 64
D_A = D_MODEL // 2
D_B = D_MODEL - D_A
GROUPS_A = 8
GROUP_A = D_A // GROUPS_A
MLP_CHUNK = 128
N_HEADS_B = 8
HEAD_DIM_B = D_B // N_HEADS_B
LEFT_CHUNKS = 8
KV_WIN = LEFT_CHUNKS * CHUNK
BAND = (LEFT_CHUNKS + 1) * CHUNK
REL_CLIP = 128
N_REL = 2 * REL_CLIP + 1
D_FF = ((8 * D_MODEL // 3 + 255) // 256) * 256
D_IN = 2 * D_A + 3 * D_B
SPLITS = [D_A, 2 * D_A, 2 * D_A + D_B, 2 * D_A + 2 * D_B]
EPS = 1e-6
NEG_INF = -1e30

kernel_name = 'hybrid_stream_gmlp_bandattn_step'


def _rmsnorm(x, g):
    xf = x.astype(jnp.float32)
    y = xf * lax.rsqrt(jnp.mean(jnp.square(xf), axis=-1, keepdims=True) + EPS)
    return (y * g.astype(jnp.float32)).astype(x.dtype)


def _swiglu(x, w_gate, w_up, w_down):
    return (jax.nn.silu(x @ w_gate) * (x @ w_up)) @ w_down


def _spatial_gate(u, v, sw, sb):
    L = v.shape[2]
    mask = jnp.tril(jnp.ones((L, L), dtype=bool))
    w = jnp.where(mask[None], sw[:, :L, :L], 0).astype(v.dtype)
    vg = v.reshape(v.shape[:3] + (GROUPS_A, GROUP_A))
    bias = jnp.transpose(sb[:, :L])[:, :, None].astype(v.dtype)
    mixed = jnp.einsum('gts,bnsgc->bntgc', w, vg) + bias
    return u * mixed.reshape(u.shape)


def _band_attention(q, k, v, rel, valid, rel_bias):
    bias = rel_bias[:, jnp.clip(rel, -REL_CLIP, REL_CLIP) + REL_CLIP].astype(jnp.float32)
    s = jnp.einsum('bnqhd,bnkhd->bnhqk', q.astype(jnp.float32), k.astype(jnp.float32)) * (HEAD_DIM_B ** -0.5) + bias
    s = jnp.where(valid[None, :, None, None, :], s, NEG_INF)
    p = jax.nn.softmax(s, axis=-1).astype(v.dtype)
    return jnp.einsum('bnhqk,bnkhd->bnqhd', p, v)


def _mixer_prompt(u, va, q, k, vb, sw, sb, rel_bias):
    b, s, _ = u.shape
    nm = s // MLP_CHUNK
    a = _spatial_gate(u.reshape(b, nm, MLP_CHUNK, D_A), va.reshape(b, nm, MLP_CHUNK, D_A), sw, sb).reshape(b, s, D_A)
    nc = s // CHUNK
    shp = (b, nc, CHUNK, N_HEADS_B, HEAD_DIM_B)
    pad = ((0, 0), (LEFT_CHUNKS, 0), (0, 0), (0, 0), (0, 0))
    kp = jnp.pad(k.reshape(shp), pad)
    vp = jnp.pad(vb.reshape(shp), pad)
    kband = jnp.concatenate([kp[:, j:j + nc] for j in range(LEFT_CHUNKS + 1)], axis=2)
    vband = jnp.concatenate([vp[:, j:j + nc] for j in range(LEFT_CHUNKS + 1)], axis=2)
    kj = jnp.arange(BAND)
    rel = jnp.arange(CHUNK)[:, None] - kj[None, :] + LEFT_CHUNKS * CHUNK
    valid = (jnp.arange(nc)[:, None] - LEFT_CHUNKS) * CHUNK + kj[None, :] >= 0
    o = _band_attention(q.reshape(shp), kband, vband, rel, valid, rel_bias).reshape(b, s, D_B)
    win = min(KV_WIN, s)
    hs = (b, s, N_HEADS_B, HEAD_DIM_B)
    return a, o, (k.reshape(hs)[:, s - win:], vb.reshape(hs)[:, s - win:])


def _mixer_sample(u, va, q, k, vb, ck, cv, sw, sb, rel_bias):
    b, n, _ = u.shape
    a = _spatial_gate(u[:, None], va[:, None], sw, sb)[:, 0]
    hs = (b, n, N_HEADS_B, HEAD_DIM_B)
    kn = k.reshape(hs)
    vn = vb.reshape(hs)
    w = ck.shape[1]
    k_all = jnp.concatenate([ck.astype(kn.dtype), kn], axis=1)
    v_all = jnp.concatenate([cv.astype(vn.dtype), vn], axis=1)
    rel = jnp.arange(n)[:, None] - (jnp.arange(w + n)[None, :] - w)
    valid = jnp.ones((1, w + n), dtype=bool)
    o = _band_attention(q.reshape(hs)[:, None], k_all[:, None], v_all[:, None], rel, valid, rel_bias)[:, 0].reshape(b, n, D_B)
    return a, o, (kn, vn, va)


def _layer(x, mixer, w_in, w_out, g_va, g_oa, g_ob, g_f1_pre, g_f1_post, g_m_pre, g_m_post, g_f2_pre, g_f2_post,
           f1_gate, f1_up, f1_down, f2_gate, f2_up, f2_down):
    h = x + 0.5 * _rmsnorm(_swiglu(_rmsnorm(x, g_f1_pre), f1_gate, f1_up, f1_down), g_f1_post)
    z = _rmsnorm(h, g_m_pre) @ w_in
    u, va, q, k, vb = jnp.split(z, SPLITS, axis=-1)
    u = jax.nn.gelu(u)
    va = _rmsnorm(jax.nn.gelu(va), g_va)
    a, o, state = mixer(u, va, q, k, vb)
    m = jnp.concatenate([_rmsnorm(a, g_oa), _rmsnorm(o, g_ob)], axis=-1) @ w_out
    h = h + _rmsnorm(m, g_m_post)
    y = h + 0.5 * _rmsnorm(_swiglu(_rmsnorm(h, g_f2_pre), f2_gate, f2_up, f2_down), g_f2_post)
    return y, state


def setup_inputs(seed: int = 0) -> dict:
    key = jax.random.key(seed)
    ks = jax.random.split(key, 32)
    nrm = lambda i, shape, scale: scale * jax.random.normal(ks[i], shape, dtype=jnp.float32)
    gain = lambda i, d: 1.0 + nrm(i, (DEPTH, d), 0.1)
    win = min(KV_WIN, PAST_LEN)
    cshape = (DEPTH, DEC_BATCH, win, N_HEADS_B, HEAD_DIM_B)
    return {
        'x_prompt': nrm(0, (BATCH, SEQ, D_MODEL), 1.0),
        'x_sample': nrm(1, (DEC_BATCH, DEC_SEQ, D_MODEL), 1.0),
        'cache_b_k': nrm(2, cshape, 1.0),
        'cache_b_v': nrm(3, cshape, 1.0),
        'w_in': nrm(4, (DEPTH, D_MODEL, D_IN), D_MODEL ** -0.5),
        'w_out': nrm(5, (DEPTH, D_MODEL, D_MODEL), D_MODEL ** -0.5),
        'sw_a': nrm(6, (DEPTH, GROUPS_A, MLP_CHUNK, MLP_CHUNK), MLP_CHUNK ** -0.5),
        'sb_a': 1.0 + nrm(7, (DEPTH, GROUPS_A, MLP_CHUNK), 0.1),
        'rel_bias': nrm(8, (DEPTH, N_HEADS_B, N_REL), 0.5),
        'g_va': gain(9, D_A),
        'g_oa': gain(10, D_A),
        'g_ob': gain(11, D_B),
        'g_f1_pre': gain(12, D_MODEL),
        'g_f1_post': gain(13, D_MODEL),
        'g_m_pre': gain(14, D_MODEL),
        'g_m_post': gain(15, D_MODEL),
        'g_f2_pre': gain(16, D_MODEL),
        'g_f2_post': gain(17, D_MODEL),
        'f1_gate': nrm(18, (DEPTH, D_MODEL, D_FF), D_MODEL ** -0.5),
        'f1_up': nrm(19, (DEPTH, D_MODEL, D_FF), D_MODEL ** -0.5),
        'f1_down': nrm(20, (DEPTH, D_FF, D_MODEL), D_FF ** -0.5),
        'f2_gate': nrm(21, (DEPTH, D_MODEL, D_FF), D_MODEL ** -0.5),
        'f2_up': nrm(22, (DEPTH, D_MODEL, D_FF), D_MODEL ** -0.5),
        'f2_down': nrm(23, (DEPTH, D_FF, D_MODEL), D_FF ** -0.5),
    }


def reference(x_prompt, x_sample, cache_b_k, cache_b_v, w_in, w_out, sw_a, sb_a, rel_bias, g_va, g_oa, g_ob,
              g_f1_pre, g_f1_post, g_m_pre, g_m_post, g_f2_pre, g_f2_post,
              f1_gate, f1_up, f1_down, f2_gate, f2_up, f2_down):
    yp, ys = x_prompt, x_sample
    pk, pv, sk, sv, sa = [], [], [], [], []
    for l in range(DEPTH):
        shared = (w_in[l], w_out[l], g_va[l], g_oa[l], g_ob[l], g_f1_pre[l], g_f1_post[l], g_m_pre[l], g_m_post[l],
                  g_f2_pre[l], g_f2_post[l], f1_gate[l], f1_up[l], f1_down[l], f2_gate[l], f2_up[l], f2_down[l])
        mp = functools.partial(_mixer_prompt, sw=sw_a[l], sb=sb_a[l], rel_bias=rel_bias[l])
        ms = functools.partial(_mixer_sample, ck=cache_b_k[l], cv=cache_b_v[l], sw=sw_a[l], sb=sb_a[l], rel_bias=rel_bias[l])
        yp, (k_p, v_p) = _layer(yp, mp, *shared)
        ys, (k_s, v_s, va_s) = _layer(ys, ms, *shared)
        pk.append(k_p)
        pv.append(v_p)
        sk.append(k_s)
        sv.append(v_s)
        sa.append(va_s)
    new_b_k_prompt = jnp.stack(pk)
    new_b_v_prompt = jnp.stack(pv)
    new_b_k_sample = jnp.stack(sk)
    new_b_v_sample = jnp.stack(sv)
    new_a_v_sample = jnp.stack(sa)
    return (yp, ys, new_b_k_prompt, new_b_v_prompt, new_b_k_sample, new_b_v_sample, new_a_v_sample)
```

```python
import functools

import jax
import jax.numpy as jnp
from jax import lax
from jax.experimental import pallas as pl
from jax.experimental.pallas import tpu as pltpu

D_MODEL = 2048
D_A = 1024
D_B = 1024
GROUPS_A = 8
GROUP_A = 128
MLP_CHUNK = 128
N_HEADS = 8
HEAD_DIM = 128
CHUNK = 64
LEFT_CHUNKS = 8
KV_WIN = LEFT_CHUNKS * CHUNK
BAND = (LEFT_CHUNKS + 1) * CHUNK
REL_CLIP = 128
D_FF = 5632
D_IN = 2 * D_A + 3 * D_B
EPS = 1e-6
NEG_INF = -1e30
SCALE = HEAD_DIM ** -0.5

F32 = jnp.float32
BF16 = jnp.bfloat16

VMEM_LIMIT_BYTES = 60 * 1024 * 1024


def _rms(x, g):
    return x * lax.rsqrt(jnp.mean(x * x, axis=-1, keepdims=True) + EPS) * g


def _params(sem):
    return pltpu.CompilerParams(dimension_semantics=sem, vmem_limit_bytes=VMEM_LIMIT_BYTES)


def _ffn_kernel(x_ref, gpre_ref, gpost_ref, wg_ref, wu_ref, wd_ref, o_ref, xn_ref, *, slab):
    j = pl.program_id(1)
    nj = pl.num_programs(1)
    tm = x_ref.shape[0]
    half = D_MODEL // 2

    @pl.when(j == 0)
    def _():
        def body(r, c):
            rows = pl.ds(pl.multiple_of(r * slab, slab), slab)
            xn_ref[rows, :] = _rms(x_ref[rows, :], gpre_ref[...]).astype(BF16)
            return c
        lax.fori_loop(0, tm // slab, body, 0)

    xn = xn_ref[...]
    g = jnp.dot(xn, wg_ref[...], preferred_element_type=F32)
    u = jnp.dot(xn, wu_ref[...], preferred_element_type=F32)
    a = (g * jax.nn.sigmoid(g) * u).astype(BF16)

    for lo in (0, half):
        d = jnp.dot(a, wd_ref[:, lo:lo + half], preferred_element_type=F32)

        @pl.when(j == 0)
        def _():
            o_ref[:, lo:lo + half] = d

        @pl.when(j > 0)
        def _():
            o_ref[:, lo:lo + half] += d

    @pl.when(j == nj - 1)
    def _():
        def body(r, c):
            rows = pl.ds(pl.multiple_of(r * slab, slab), slab)
            o_ref[rows, :] = x_ref[rows, :] + 0.5 * _rms(o_ref[rows, :], gpost_ref[...])
            return c
        lax.fori_loop(0, tm // slab, body, 0)


def _ffn(x, gpre, gpost, wg, wu, wd, *, tm, tf):
    m = x.shape[0]
    slab = min(tm, 128)
    return pl.pallas_call(
        functools.partial(_ffn_kernel, slab=slab),
        grid=(m // tm, D_FF // tf),
        in_specs=[
            pl.BlockSpec((tm, D_MODEL), lambda i, j: (i, 0)),
            pl.BlockSpec((1, D_MODEL), lambda i, j: (0, 0)),
            pl.BlockSpec((1, D_MODEL), lambda i, j: (0, 0)),
            pl.BlockSpec((D_MODEL, tf), lambda i, j: (0, j)),
            pl.BlockSpec((D_MODEL, tf), lambda i, j: (0, j)),
            pl.BlockSpec((tf, D_MODEL), lambda i, j: (j, 0)),
        ],
        out_specs=pl.BlockSpec((tm, D_MODEL), lambda i, j: (i, 0)),
        out_shape=jax.ShapeDtypeStruct((m, D_MODEL), F32),
        scratch_shapes=[pltpu.VMEM((tm, D_MODEL), BF16)],
        compiler_params=_params(("parallel", "arbitrary")),
        name="ffn",
    )(x, gpre, gpost, wg, wu, wd)


def _inproj_kernel(h_ref, gpre_ref, w_ref, gva_ref, u_ref, va_ref, q_ref, k_ref, v_ref, *rest, slab, tail):
    if tail:
        kt_ref, vt_ref, hn_ref = rest
    else:
        (hn_ref,) = rest
    i = pl.program_id(0)
    c = pl.program_id(1)
    ni = pl.num_programs(0)
    tm = h_ref.shape[0]

    @pl.when(c == 0)
    def _():
        def body(r, cc):
            rows = pl.ds(pl.multiple_of(r * slab, slab), slab)
            hn_ref[rows, :] = _rms(h_ref[rows, :], gpre_ref[...]).astype(BF16)
            return cc
        lax.fori_loop(0, tm // slab, body, 0)

    z = jnp.dot(hn_ref[...], w_ref[...], preferred_element_type=F32)

    @pl.when(c == 0)
    def _():
        u_ref[...] = jax.nn.gelu(z)

    @pl.when(c == 1)
    def _():
        va_ref[...] = _rms(jax.nn.gelu(z), gva_ref[...]).astype(va_ref.dtype)

    @pl.when(c == 2)
    def _():
        q_ref[...] = z.astype(q_ref.dtype)

    @pl.when(c == 3)
    def _():
        k_ref[...] = z.astype(k_ref.dtype)

    @pl.when(c == 4)
    def _():
        v_ref[...] = z.astype(v_ref.dtype)

    if tail:
        @pl.when(jnp.logical_and(c == 3, i == ni - 1))
        def _():
            kt_ref[...] = z[tm - KV_WIN:, :]

        @pl.when(jnp.logical_and(c == 4, i == ni - 1))
        def _():
            vt_ref[...] = z[tm - KV_WIN:, :]


def _inproj(h, gpre, w_in, gva, *, tm, va_dtype, kv_dtype, tail):
    m = h.shape[0]
    slab = min(tm, 128)
    row_blk = lambda i, c: (i, 0)
    const = lambda i, c: (0, 0)
    out_specs = [pl.BlockSpec((tm, D_A), row_blk)] * 5
    out_shape = [
        jax.ShapeDtypeStruct((m, D_A), F32),
        jax.ShapeDtypeStruct((m, D_A), va_dtype),
        jax.ShapeDtypeStruct((m, D_B), BF16),
        jax.ShapeDtypeStruct((m, D_B), kv_dtype),
        jax.ShapeDtypeStruct((m, D_B), kv_dtype),
    ]
    if tail:
        assert tm >= KV_WIN
        out_specs += [pl.BlockSpec((KV_WIN, D_B), const)] * 2
        out_shape += [jax.ShapeDtypeStruct((KV_WIN, D_B), F32)] * 2
    return pl.pallas_call(
        functools.partial(_inproj_kernel, slab=slab, tail=tail),
        grid=(m // tm, 5),
        in_specs=[
            pl.BlockSpec((tm, D_MODEL), row_blk),
            pl.BlockSpec((1, D_MODEL), const),
            pl.BlockSpec((D_MODEL, D_A), lambda i, c: (0, c)),
            pl.BlockSpec((1, D_A), const),
        ],
        out_specs=out_specs,
        out_shape=out_shape,
        scratch_shapes=[pltpu.VMEM((tm, D_MODEL), BF16)],
        compiler_params=_params(("arbitrary", "arbitrary")),
        name="inproj",
    )(h, gpre, w_in, gva)


def _mask_spatial_weights(sw_ref, wm_ref):
    t = lax.broadcasted_iota(jnp.int32, (MLP_CHUNK, MLP_CHUNK), 0)
    s = lax.broadcasted_iota(jnp.int32, (MLP_CHUNK, MLP_CHUNK), 1)
    for g in range(GROUPS_A):
        wm_ref[g] = jnp.where(s <= t, sw_ref[g], 0.0).astype(BF16)


def _attend(qh, kh, vh, bias, valid):
    s = lax.dot_general(qh, kh, (((1,), (1,)), ((), ())), preferred_element_type=F32) * SCALE + bias
    if valid is not None:
        s = jnp.where(valid, s, NEG_INF)
    e = jnp.exp(s - jnp.max(s, axis=-1, keepdims=True))
    p = e / jnp.sum(e, axis=-1, keepdims=True)
    return jnp.dot(p.astype(BF16), vh, preferred_element_type=F32)


def _mixer_prompt_kernel(u_ref, va_ref, q_ref, k_ref, v_ref, sw_ref, sbt_ref, bias_ref, goa_ref, gob_ref,
                         cat_ref, wm_ref, kbuf, vbuf, a_scr, o_scr):
    i = pl.program_id(0)
    tb = u_ref.shape[0]

    @pl.when(i == 0)
    def _():
        _mask_spatial_weights(sw_ref, wm_ref)
        kbuf[0:KV_WIN, :] = jnp.zeros((KV_WIN, D_B), BF16)
        vbuf[0:KV_WIN, :] = jnp.zeros((KV_WIN, D_B), BF16)

    @pl.when(i > 0)
    def _():
        kbuf[0:KV_WIN, :] = kbuf[tb:tb + KV_WIN, :]
        vbuf[0:KV_WIN, :] = vbuf[tb:tb + KV_WIN, :]

    kbuf[KV_WIN:KV_WIN + tb, :] = k_ref[...]
    vbuf[KV_WIN:KV_WIN + tb, :] = v_ref[...]

    def gmlp_body(n, carry):
        rows = pl.ds(pl.multiple_of(n * MLP_CHUNK, MLP_CHUNK), MLP_CHUNK)
        ssq = jnp.zeros((MLP_CHUNK, 1), F32)
        for g in range(GROUPS_A):
            cols = slice(g * GROUP_A, (g + 1) * GROUP_A)
            mixed = jnp.dot(wm_ref[g], va_ref[rows, cols], preferred_element_type=F32) + sbt_ref[:, g:g + 1]
            a = u_ref[rows, cols] * mixed
            a_scr[:, cols] = a
            ssq = ssq + jnp.sum(a * a, axis=-1, keepdims=True)
        inv = lax.rsqrt(ssq * (1.0 / D_A) + EPS)
        cat_ref[rows, 0:D_A] = (a_scr[...] * inv * goa_ref[...]).astype(BF16)
        return carry

    lax.fori_loop(0, tb // MLP_CHUNK, gmlp_body, 0)

    kj = lax.broadcasted_iota(jnp.int32, (CHUNK, BAND), 1)

    def attn_body(c, carry):
        q0 = pl.multiple_of(c * CHUNK, CHUNK)
        qrows = pl.ds(q0, CHUNK)
        krows = pl.ds(q0, BAND)
        valid = kj + (q0 + i * tb - KV_WIN) >= 0
        ssq = jnp.zeros((CHUNK, 1), F32)
        for h in range(N_HEADS):
            cols = slice(h * HEAD_DIM, (h + 1) * HEAD_DIM)
            o = _attend(q_ref[qrows, cols], kbuf[krows, cols], vbuf[krows, cols], bias_ref[h], valid)
            o_scr[:, cols] = o
            ssq = ssq + jnp.sum(o * o, axis=-1, keepdims=True)
        inv = lax.rsqrt(ssq * (1.0 / D_B) + EPS)
        cat_ref[qrows, D_A:D_A + D_B] = (o_scr[...] * inv * gob_ref[...]).astype(BF16)
        return carry

    lax.fori_loop(0, tb // CHUNK, attn_body, 0)


def _mixer_prompt(u, va, q, k, v, sw, sbt, bias, goa, gob, *, tb):
    m = u.shape[0]
    assert tb >= KV_WIN
    row_blk = lambda i: (i, 0)
    const2 = lambda i: (0, 0)
    const3 = lambda i: (0, 0, 0)
    return pl.pallas_call(
        _mixer_prompt_kernel,
        grid=(m // tb,),
        in_specs=[
            pl.BlockSpec((tb, D_A), row_blk),
            pl.BlockSpec((tb, D_A), row_blk),
            pl.BlockSpec((tb, D_B), row_blk),
            pl.BlockSpec((tb, D_B), row_blk),
            pl.BlockSpec((tb, D_B), row_blk),
            pl.BlockSpec((GROUPS_A, MLP_CHUNK, MLP_CHUNK), const3),
            pl.BlockSpec((MLP_CHUNK, GROUPS_A), const2),
            pl.BlockSpec((N_HEADS, CHUNK, BAND), const3),
            pl.BlockSpec((1, D_A), const2),
            pl.BlockSpec((1, D_B), const2),
        ],
        out_specs=pl.BlockSpec((tb, D_MODEL), row_blk),
        out_shape=jax.ShapeDtypeStruct((m, D_MODEL), BF16),
        scratch_shapes=[
            pltpu.VMEM((GROUPS_A, MLP_CHUNK, MLP_CHUNK), BF16),
            pltpu.VMEM((KV_WIN + tb, D_B), BF16),
            pltpu.VMEM((KV_WIN + tb, D_B), BF16),
            pltpu.VMEM((MLP_CHUNK, D_A), F32),
            pltpu.VMEM((CHUNK, D_B), F32),
        ],
        compiler_params=_params(("arbitrary",)),
        name="mixer_prompt",
    )(u, va, q, k, v, sw, sbt, bias, goa, gob)


def _mixer_sample_kernel(u_ref, va_ref, q_ref, k_ref, v_ref, ck_ref, cv_ref, sw_ref, sbt_ref, bias_ref,
                         goa_ref, gob_ref, cat_ref, wm_ref, vap, kbuf, vbuf, a_scr, o_scr):
    b = pl.program_id(0)
    n = u_ref.shape[0]
    w = ck_ref.shape[1]

    @pl.when(b == 0)
    def _():
        _mask_spatial_weights(sw_ref, wm_ref)
        vap[...] = jnp.zeros(vap.shape, BF16)

    vap[0:n, :] = va_ref[...].astype(BF16)
    kbuf[0:w, :] = ck_ref[0].astype(BF16)
    vbuf[0:w, :] = cv_ref[0].astype(BF16)
    kbuf[w:w + n, :] = k_ref[...].astype(BF16)
    vbuf[w:w + n, :] = v_ref[...].astype(BF16)

    ssq = jnp.zeros((n, 1), F32)
    for g in range(GROUPS_A):
        cols = slice(g * GROUP_A, (g + 1) * GROUP_A)
        mixed = jnp.dot(wm_ref[g, 0:n, :], vap[:, cols], preferred_element_type=F32) + sbt_ref[0:n, g:g + 1]
        a = u_ref[:, cols] * mixed
        a_scr[:, cols] = a
        ssq = ssq + jnp.sum(a * a, axis=-1, keepdims=True)
    inv = lax.rsqrt(ssq * (1.0 / D_A) + EPS)
    cat_ref[:, 0:D_A] = (a_scr[...] * inv * goa_ref[...]).astype(BF16)

    ssq = jnp.zeros((n, 1), F32)
    for h in range(N_HEADS):
        cols = slice(h * HEAD_DIM, (h + 1) * HEAD_DIM)
        o = _attend(q_ref[:, cols], kbuf[:, cols], vbuf[:, cols], bias_ref[h], None)
        o_scr[:, cols] = o
        ssq = ssq + jnp.sum(o * o, axis=-1, keepdims=True)
    inv = lax.rsqrt(ssq * (1.0 / D_B) + EPS)
    cat_ref[:, D_A:D_A + D_B] = (o_scr[...] * inv * gob_ref[...]).astype(BF16)


def _mixer_sample(u, va, q, k, v, ck, cv, sw, sbt, bias, goa, gob, *, n):
    m = u.shape[0]
    nb, w, _ = ck.shape
    row_blk = lambda b: (b, 0)
    const2 = lambda b: (0, 0)
    const3 = lambda b: (0, 0, 0)
    return pl.pallas_call(
        _mixer_sample_kernel,
        grid=(nb,),
        in_specs=[
            pl.BlockSpec((n, D_A), row_blk),
            pl.BlockSpec((n, D_A), row_blk),
            pl.BlockSpec((n, D_B), row_blk),
            pl.BlockSpec((n, D_B), row_blk),
            pl.BlockSpec((n, D_B), row_blk),
            pl.BlockSpec((1, w, D_B), lambda b: (b, 0, 0)),
            pl.BlockSpec((1, w, D_B), lambda b: (b, 0, 0)),
            pl.BlockSpec((GROUPS_A, MLP_CHUNK, MLP_CHUNK), const3),
            pl.BlockSpec((MLP_CHUNK, GROUPS_A), const2),
            pl.BlockSpec((N_HEADS, n, w + n), const3),
            pl.BlockSpec((1, D_A), const2),
            pl.BlockSpec((1, D_B), const2),
        ],
        out_specs=pl.BlockSpec((n, D_MODEL), row_blk),
        out_shape=jax.ShapeDtypeStruct((m, D_MODEL), BF16),
        scratch_shapes=[
            pltpu.VMEM((GROUPS_A, MLP_CHUNK, MLP_CHUNK), BF16),
            pltpu.VMEM((MLP_CHUNK, D_A), BF16),
            pltpu.VMEM((w + n, D_B), BF16),
            pltpu.VMEM((w + n, D_B), BF16),
            pltpu.VMEM((n, D_A), F32),
            pltpu.VMEM((n, D_B), F32),
        ],
        compiler_params=_params(("arbitrary",)),
        name="mixer_sample",
    )(u, va, q, k, v, ck, cv, sw, sbt, bias, goa, gob)


def _outproj_kernel(cat_ref, h_ref, w_ref, g_ref, o_ref):
    m = jnp.dot(cat_ref[...], w_ref[...], preferred_element_type=F32)
    o_ref[...] = h_ref[...] + _rms(m, g_ref[...])


def _outproj(cat, h, w_out, g, *, tm):
    m = cat.shape[0]
    return pl.pallas_call(
        _outproj_kernel,
        grid=(m // tm,),
        in_specs=[
            pl.BlockSpec((tm, D_MODEL), lambda i: (i, 0)),
            pl.BlockSpec((tm, D_MODEL), lambda i: (i, 0)),
            pl.BlockSpec((D_MODEL, D_MODEL), lambda i: (0, 0)),
            pl.BlockSpec((1, D_MODEL), lambda i: (0, 0)),
        ],
        out_specs=pl.BlockSpec((tm, D_MODEL), lambda i: (i, 0)),
        out_shape=jax.ShapeDtypeStruct((m, D_MODEL), F32),
        compiler_params=_params(("parallel",)),
        name="outproj",
    )(cat, h, w_out, g)


def _rel_bias_table(rel_bias, lq, lk):
    rel = jnp.arange(lq)[:, None] - jnp.arange(lk)[None, :] + KV_WIN
    return rel_bias[:, jnp.clip(rel, -REL_CLIP, REL_CLIP) + REL_CLIP].astype(F32)


def kernel(x_prompt, x_sample, cache_b_k, cache_b_v, w_in, w_out, sw_a, sb_a, rel_bias, g_va, g_oa, g_ob,
           g_f1_pre, g_f1_post, g_m_pre, g_m_post, g_f2_pre, g_f2_post,
           f1_gate, f1_up, f1_down, f2_gate, f2_up, f2_down):
    depth = w_in.shape[0]
    assert depth == 1
    bp, sp, _ = x_prompt.shape
    bs, ns, _ = x_sample.shape
    assert bp == 1
    l = 0
    row = lambda g: g[l].reshape(1, -1)
    w_in_b = w_in[l].astype(BF16)
    w_out_b = w_out[l].astype(BF16)
    f1 = (f1_gate[l].astype(BF16), f1_up[l].astype(BF16), f1_down[l].astype(BF16))
    f2 = (f2_gate[l].astype(BF16), f2_up[l].astype(BF16), f2_down[l].astype(BF16))
    sbt = jnp.transpose(sb_a[l])
    bias_p = _rel_bias_table(rel_bias[l], CHUNK, BAND)
    wc = cache_b_k.shape[2]
    bias_s = _rel_bias_table(rel_bias[l], ns, wc + ns)

    xp = x_prompt.reshape(sp, D_MODEL)
    h = _ffn(xp, row(g_f1_pre), row(g_f1_post), *f1, tm=1024, tf=512)
    u, va, q, k, v, kt, vt = _inproj(h, row(g_m_pre), w_in_b, row(g_va), tm=512,
                                     va_dtype=BF16, kv_dtype=BF16, tail=True)
    cat = _mixer_prompt(u, va, q, k, v, sw_a[l], sbt, bias_p, row(g_oa), row(g_ob), tb=512)
    h2 = _outproj(cat, h, w_out_b, row(g_m_post), tm=512)
    yp = _ffn(h2, row(g_f2_pre), row(g_f2_post), *f2, tm=1024, tf=512)

    ms = bs * ns
    xs = x_sample.reshape(ms, D_MODEL)
    hs = _ffn(xs, row(g_f1_pre), row(g_f1_post), *f1, tm=ms, tf=512)
    us, vas, qs, ks, vs = _inproj(hs, row(g_m_pre), w_in_b, row(g_va), tm=ms,
                                  va_dtype=F32, kv_dtype=F32, tail=False)
    ck = cache_b_k[l].reshape(bs, wc, D_B)
    cv = cache_b_v[l].reshape(bs, wc, D_B)
    cats = _mixer_sample(us, vas, qs, ks, vs, ck, cv, sw_a[l], sbt, bias_s, row(g_oa), row(g_ob), n=ns)
    hs2 = _outproj(cats, hs, w_out_b, row(g_m_post), tm=ms)
    ys = _ffn(hs2, row(g_f2_pre), row(g_f2_post), *f2, tm=ms, tf=512)

    hd = (N_HEADS, HEAD_DIM)
    return (
        yp.reshape(bp, sp, D_MODEL),
        ys.reshape(bs, ns, D_MODEL),
        kt.reshape(depth, bp, KV_WIN, *hd),
        vt.reshape(depth, bp, KV_WIN, *hd),
        ks.reshape(depth, bs, ns, *hd),
        vs.reshape(depth, bs, ns, *hd),
        vas.reshape(depth, bs, ns, D_A),
    )
```

```python
import functools

import jax
import jax.numpy as jnp
from jax import lax
from jax.experimental import pallas as pl
from jax.experimental.pallas import tpu as pltpu

D_MODEL = 2048
D_A = 1024
D_B = 1024
GROUPS_A = 8
GROUP_A = 128
MLP_CHUNK = 128
N_HEADS = 8
HEAD_DIM = 128
CHUNK = 64
LEFT_CHUNKS = 8
KV_WIN = LEFT_CHUNKS * CHUNK
BAND = (LEFT_CHUNKS + 1) * CHUNK
GROUP_CHUNKS = 4
GROUP_Q = GROUP_CHUNKS * CHUNK
GROUP_K = GROUP_Q + KV_WIN
REL_CLIP = 128
D_FF = 5632
D_IN = 2 * D_A + 3 * D_B
EPS = 1e-6
NEG_INF = -1e30
SCALE = HEAD_DIM ** -0.5
LOG2E = 1.4426950408889634

F32 = jnp.float32
BF16 = jnp.bfloat16

VMEM_LIMIT_BYTES = 60 * 1024 * 1024
FFN_ROW_CHUNK = 256


def _rms(x, g):
    return x * lax.rsqrt(jnp.mean(x * x, axis=-1, keepdims=True) + EPS) * g


def _params(sem):
    return pltpu.CompilerParams(dimension_semantics=sem, vmem_limit_bytes=VMEM_LIMIT_BYTES)


def _ffn_kernel(x_ref, gpre_ref, gpost_ref, wg_ref, wu_ref, wd_ref, o_ref, xn_ref, *, slab, rc):
    j = pl.program_id(1)
    nj = pl.num_programs(1)
    tm = x_ref.shape[0]

    @pl.when(j == 0)
    def _():
        def body(r, c):
            rows = pl.ds(pl.multiple_of(r * slab, slab), slab)
            xn_ref[rows, :] = _rms(x_ref[rows, :], gpre_ref[...]).astype(BF16)
            o_ref[rows, :] = jnp.zeros((slab, D_MODEL), F32)
            return c
        lax.fori_loop(0, tm // slab, body, 0)

    for r in range(tm // rc):
        rows = slice(r * rc, (r + 1) * rc)
        xn = xn_ref[rows, :]
        g = jnp.dot(xn, wg_ref[...], preferred_element_type=F32)
        u = jnp.dot(xn, wu_ref[...], preferred_element_type=F32)
        a = (g * jax.nn.sigmoid(g) * u).astype(BF16)
        o_ref[rows, :] += jnp.dot(a, wd_ref[...], preferred_element_type=F32)

    @pl.when(j == nj - 1)
    def _():
        def body(r, c):
            rows = pl.ds(pl.multiple_of(r * slab, slab), slab)
            o_ref[rows, :] = x_ref[rows, :] + 0.5 * _rms(o_ref[rows, :], gpost_ref[...])
            return c
        lax.fori_loop(0, tm // slab, body, 0)


def _ffn(x, gpre, gpost, wg, wu, wd, *, tm, tf):
    m = x.shape[0]
    slab = min(tm, 128)
    return pl.pallas_call(
        functools.partial(_ffn_kernel, slab=slab, rc=min(tm, FFN_ROW_CHUNK)),
        grid=(m // tm, D_FF // tf),
        in_specs=[
            pl.BlockSpec((tm, D_MODEL), lambda i, j: (i, 0)),
            pl.BlockSpec((1, D_MODEL), lambda i, j: (0, 0)),
            pl.BlockSpec((1, D_MODEL), lambda i, j: (0, 0)),
            pl.BlockSpec((D_MODEL, tf), lambda i, j: (0, j)),
            pl.BlockSpec((D_MODEL, tf), lambda i, j: (0, j)),
            pl.BlockSpec((tf, D_MODEL), lambda i, j: (j, 0)),
        ],
        out_specs=pl.BlockSpec((tm, D_MODEL), lambda i, j: (i, 0)),
        out_shape=jax.ShapeDtypeStruct((m, D_MODEL), F32),
        scratch_shapes=[pltpu.VMEM((tm, D_MODEL), BF16)],
        compiler_params=_params(("parallel", "arbitrary")),
        name="ffn",
    )(x, gpre, gpost, wg, wu, wd)


def _inproj_kernel(h_ref, gpre_ref, w_ref, gva_ref, u_ref, va_ref, q_ref, k_ref, v_ref, *rest, slab, tail):
    if tail:
        kt_ref, vt_ref, hn_ref = rest
    else:
        (hn_ref,) = rest
    i = pl.program_id(0)
    c = pl.program_id(1)
    ni = pl.num_programs(0)
    tm = h_ref.shape[0]

    @pl.when(c == 0)
    def _():
        def body(r, cc):
            rows = pl.ds(pl.multiple_of(r * slab, slab), slab)
            hn_ref[rows, :] = _rms(h_ref[rows, :], gpre_ref[...]).astype(BF16)
            return cc
        lax.fori_loop(0, tm // slab, body, 0)

    z = jnp.dot(hn_ref[...], w_ref[...], preferred_element_type=F32)

    @pl.when(c == 0)
    def _():
        u_ref[...] = jax.nn.gelu(z)

    @pl.when(c == 1)
    def _():
        va_ref[...] = _rms(jax.nn.gelu(z), gva_ref[...]).astype(va_ref.dtype)

    @pl.when(c == 2)
    def _():
        q_ref[...] = z.astype(q_ref.dtype)

    @pl.when(c == 3)
    def _():
        k_ref[...] = z.astype(k_ref.dtype)

    @pl.when(c == 4)
    def _():
        v_ref[...] = z.astype(v_ref.dtype)

    if tail:
        @pl.when(jnp.logical_and(c == 3, i == ni - 1))
        def _():
            kt_ref[...] = z[tm - KV_WIN:, :]

        @pl.when(jnp.logical_and(c == 4, i == ni - 1))
        def _():
            vt_ref[...] = z[tm - KV_WIN:, :]


def _inproj(h, gpre, w_in, gva, *, tm, va_dtype, kv_dtype, tail):
    m = h.shape[0]
    slab = min(tm, 128)
    row_blk = lambda i, c: (i, 0)
    const = lambda i, c: (0, 0)
    out_specs = [pl.BlockSpec((tm, D_A), row_blk)] * 5
    out_shape = [
        jax.ShapeDtypeStruct((m, D_A), F32),
        jax.ShapeDtypeStruct((m, D_A), va_dtype),
        jax.ShapeDtypeStruct((m, D_B), BF16),
        jax.ShapeDtypeStruct((m, D_B), kv_dtype),
        jax.ShapeDtypeStruct((m, D_B), kv_dtype),
    ]
    if tail:
        assert tm >= KV_WIN
        out_specs += [pl.BlockSpec((KV_WIN, D_B), const)] * 2
        out_shape += [jax.ShapeDtypeStruct((KV_WIN, D_B), F32)] * 2
    return pl.pallas_call(
        functools.partial(_inproj_kernel, slab=slab, tail=tail),
        grid=(m // tm, 5),
        in_specs=[
            pl.BlockSpec((tm, D_MODEL), row_blk),
            pl.BlockSpec((1, D_MODEL), const),
            pl.BlockSpec((D_MODEL, D_A), lambda i, c: (0, c)),
            pl.BlockSpec((1, D_A), const),
        ],
        out_specs=out_specs,
        out_shape=out_shape,
        scratch_shapes=[pltpu.VMEM((tm, D_MODEL), BF16)],
        compiler_params=_params(("arbitrary", "arbitrary")),
        name="inproj",
    )(h, gpre, w_in, gva)


def _mask_spatial_weights(sw_ref, wm_ref):
    t = lax.broadcasted_iota(jnp.int32, (MLP_CHUNK, MLP_CHUNK), 0)
    s = lax.broadcasted_iota(jnp.int32, (MLP_CHUNK, MLP_CHUNK), 1)
    for g in range(GROUPS_A):
        wm_ref[g] = jnp.where(s <= t, sw_ref[g], 0.0).astype(BF16)


def _attend(qh, kh, vh, bias, valid):
    s = lax.dot_general(qh, kh, (((1,), (1,)), ((), ())), preferred_element_type=F32) * SCALE + bias
    if valid is not None:
        s = jnp.where(valid, s, NEG_INF)
    e = jnp.exp(s - jnp.max(s, axis=-1, keepdims=True))
    p = e / jnp.sum(e, axis=-1, keepdims=True)
    return jnp.dot(p.astype(BF16), vh, preferred_element_type=F32)


def _gmlp_block(u_ref, va_ref, wm_ref, sbb_ref, goa_ref, cat_ref, a_scr):
    nchunk = u_ref.shape[0] // MLP_CHUNK
    chunk_rows = [slice(n * MLP_CHUNK, (n + 1) * MLP_CHUNK) for n in range(nchunk)]
    for g in range(GROUPS_A):
        cols = slice(g * GROUP_A, (g + 1) * GROUP_A)
        v_all = jnp.concatenate([va_ref[rows, cols] for rows in chunk_rows], axis=1)
        mixed = jnp.dot(wm_ref[g], v_all, preferred_element_type=F32)
        for n, rows in enumerate(chunk_rows):
            a_scr[rows, cols] = u_ref[rows, cols] * (mixed[:, n * GROUP_A:(n + 1) * GROUP_A] + sbb_ref[g])
    for rows in chunk_rows:
        cat_ref[rows, 0:D_A] = _rms(a_scr[rows, :], goa_ref[...]).astype(BF16)


def _attn_groups(q_ref, kbuf, vbuf, bias_ref, gob_ref, cat_ref, o_scr, *, first):
    tb = q_ref.shape[0]
    for gq in range(tb // GROUP_Q):
        q0 = gq * GROUP_Q
        qrows = slice(q0, q0 + GROUP_Q)
        krows = slice(q0, q0 + GROUP_K)
        if first:
            valid = lax.broadcasted_iota(jnp.int32, (GROUP_Q, GROUP_K), 1) >= KV_WIN - q0
        for h in range(N_HEADS):
            cols = slice(h * HEAD_DIM, (h + 1) * HEAD_DIM)
            s2 = lax.dot_general(q_ref[qrows, cols], kbuf[krows, cols], (((1,), (1,)), ((), ())),
                                 preferred_element_type=F32) * (SCALE * LOG2E) + bias_ref[h]
            if first:
                s2 = jnp.where(valid, s2, NEG_INF)
            e = jnp.exp2(s2 - jnp.max(s2, axis=-1, keepdims=True)).astype(BF16)
            nd = jnp.dot(e, vbuf[krows, 2 * h * HEAD_DIM:2 * (h + 1) * HEAD_DIM], preferred_element_type=F32)
            o_scr[:, cols] = nd[:, 0:HEAD_DIM] * (1.0 / nd[:, HEAD_DIM:2 * HEAD_DIM])
        cat_ref[qrows, D_A:D_A + D_B] = _rms(o_scr[...], gob_ref[...]).astype(BF16)


def _mixer_prompt_kernel(u_ref, va_ref, q_ref, k_ref, v_ref, sw_ref, sbb_ref, bias_ref, goa_ref, gob_ref,
                         cat_ref, wm_ref, kbuf, vbuf, a_scr, o_scr):
    i = pl.program_id(0)
    tb = u_ref.shape[0]

    @pl.when(i == 0)
    def _():
        _mask_spatial_weights(sw_ref, wm_ref)
        kbuf[0:KV_WIN, :] = jnp.zeros((KV_WIN, D_B), BF16)
        for h in range(N_HEADS):
            lo = 2 * h * HEAD_DIM
            vbuf[0:KV_WIN, lo:lo + HEAD_DIM] = jnp.zeros((KV_WIN, HEAD_DIM), BF16)
            vbuf[:, lo + HEAD_DIM:lo + 2 * HEAD_DIM] = jnp.ones((KV_WIN + tb, HEAD_DIM), BF16)

    @pl.when(i > 0)
    def _():
        kbuf[0:KV_WIN, :] = kbuf[tb:tb + KV_WIN, :]
        for h in range(N_HEADS):
            lo = 2 * h * HEAD_DIM
            vbuf[0:KV_WIN, lo:lo + HEAD_DIM] = vbuf[tb:tb + KV_WIN, lo:lo + HEAD_DIM]

    kbuf[KV_WIN:KV_WIN + tb, :] = k_ref[...]
    for h in range(N_HEADS):
        vbuf[KV_WIN:KV_WIN + tb, 2 * h * HEAD_DIM:(2 * h + 1) * HEAD_DIM] = v_ref[:, h * HEAD_DIM:(h + 1) * HEAD_DIM]

    _gmlp_block(u_ref, va_ref, wm_ref, sbb_ref, goa_ref, cat_ref, a_scr)

    @pl.when(i == 0)
    def _():
        _attn_groups(q_ref, kbuf, vbuf, bias_ref, gob_ref, cat_ref, o_scr, first=True)

    @pl.when(i > 0)
    def _():
        _attn_groups(q_ref, kbuf, vbuf, bias_ref, gob_ref, cat_ref, o_scr, first=False)


def _mixer_prompt(u, va, q, k, v, sw, sbb, bias, goa, gob, *, tb):
    m = u.shape[0]
    assert tb >= KV_WIN and tb % GROUP_Q == 0
    row_blk = lambda i: (i, 0)
    const2 = lambda i: (0, 0)
    const3 = lambda i: (0, 0, 0)
    return pl.pallas_call(
        _mixer_prompt_kernel,
        grid=(m // tb,),
        in_specs=[
            pl.BlockSpec((tb, D_A), row_blk),
            pl.BlockSpec((tb, D_A), row_blk),
            pl.BlockSpec((tb, D_B), row_blk),
            pl.BlockSpec((tb, D_B), row_blk),
            pl.BlockSpec((tb, D_B), row_blk),
            pl.BlockSpec((GROUPS_A, MLP_CHUNK, MLP_CHUNK), const3),
            pl.BlockSpec((GROUPS_A, MLP_CHUNK, GROUP_A), const3),
            pl.BlockSpec((N_HEADS, GROUP_Q, GROUP_K), const3),
            pl.BlockSpec((1, D_A), const2),
            pl.BlockSpec((1, D_B), const2),
        ],
        out_specs=pl.BlockSpec((tb, D_MODEL), row_blk),
        out_shape=jax.ShapeDtypeStruct((m, D_MODEL), BF16),
        scratch_shapes=[
            pltpu.VMEM((GROUPS_A, MLP_CHUNK, MLP_CHUNK), BF16),
            pltpu.VMEM((KV_WIN + tb, D_B), BF16),
            pltpu.VMEM((KV_WIN + tb, 2 * D_B), BF16),
            pltpu.VMEM((tb, D_A), F32),
            pltpu.VMEM((GROUP_Q, D_B), F32),
        ],
        compiler_params=_params(("arbitrary",)),
        name="mixer_prompt",
    )(u, va, q, k, v, sw, sbb, bias, goa, gob)


def _mixer_sample_kernel(u_ref, va_ref, q_ref, k_ref, v_ref, ck_ref, cv_ref, sw_ref, sbb_ref, bias_ref,
                         goa_ref, gob_ref, cat_ref, wm_ref, vap, kbuf, vbuf, a_scr, o_scr):
    b = pl.program_id(0)
    n = u_ref.shape[0]
    w = ck_ref.shape[1]

    @pl.when(b == 0)
    def _():
        _mask_spatial_weights(sw_ref, wm_ref)
        vap[...] = jnp.zeros(vap.shape, BF16)

    vap[0:n, :] = va_ref[...].astype(BF16)
    kbuf[0:w, :] = ck_ref[0].astype(BF16)
    vbuf[0:w, :] = cv_ref[0].astype(BF16)
    kbuf[w:w + n, :] = k_ref[...].astype(BF16)
    vbuf[w:w + n, :] = v_ref[...].astype(BF16)

    ssq = jnp.zeros((n, 1), F32)
    for g in range(GROUPS_A):
        cols = slice(g * GROUP_A, (g + 1) * GROUP_A)
        mixed = jnp.dot(wm_ref[g, 0:n, :], vap[:, cols], preferred_element_type=F32) + sbb_ref[g, 0:n, :]
        a = u_ref[:, cols] * mixed
        a_scr[:, cols] = a
        ssq = ssq + jnp.sum(a * a, axis=-1, keepdims=True)
    inv = lax.rsqrt(ssq * (1.0 / D_A) + EPS)
    cat_ref[:, 0:D_A] = (a_scr[...] * inv * goa_ref[...]).astype(BF16)

    ssq = jnp.zeros((n, 1), F32)
    for h in range(N_HEADS):
        cols = slice(h * HEAD_DIM, (h + 1) * HEAD_DIM)
        o = _attend(q_ref[:, cols], kbuf[:, cols], vbuf[:, cols], bias_ref[h], None)
        o_scr[:, cols] = o
        ssq = ssq + jnp.sum(o * o, axis=-1, keepdims=True)
    inv = lax.rsqrt(ssq * (1.0 / D_B) + EPS)
    cat_ref[:, D_A:D_A + D_B] = (o_scr[...] * inv * gob_ref[...]).astype(BF16)


def _mixer_sample(u, va, q, k, v, ck, cv, sw, sbb, bias, goa, gob, *, n):
    m = u.shape[0]
    nb, w, _ = ck.shape
    row_blk = lambda b: (b, 0)
    const2 = lambda b: (0, 0)
    const3 = lambda b: (0, 0, 0)
    return pl.pallas_call(
        _mixer_sample_kernel,
        grid=(nb,),
        in_specs=[
            pl.BlockSpec((n, D_A), row_blk),
            pl.BlockSpec((n, D_A), row_blk),
            pl.BlockSpec((n, D_B), row_blk),
            pl.BlockSpec((n, D_B), row_blk),
            pl.BlockSpec((n, D_B), row_blk),
            pl.BlockSpec((1, w, D_B), lambda b: (b, 0, 0)),
            pl.BlockSpec((1, w, D_B), lambda b: (b, 0, 0)),
            pl.BlockSpec((GROUPS_A, MLP_CHUNK, MLP_CHUNK), const3),
            pl.BlockSpec((GROUPS_A, MLP_CHUNK, GROUP_A), const3),
            pl.BlockSpec((N_HEADS, n, w + n), const3),
            pl.BlockSpec((1, D_A), const2),
            pl.BlockSpec((1, D_B), const2),
        ],
        out_specs=pl.BlockSpec((n, D_MODEL), row_blk),
        out_shape=jax.ShapeDtypeStruct((m, D_MODEL), BF16),
        scratch_shapes=[
            pltpu.VMEM((GROUPS_A, MLP_CHUNK, MLP_CHUNK), BF16),
            pltpu.VMEM((MLP_CHUNK, D_A), BF16),
            pltpu.VMEM((w + n, D_B), BF16),
            pltpu.VMEM((w + n, D_B), BF16),
            pltpu.VMEM((n, D_A), F32),
            pltpu.VMEM((n, D_B), F32),
        ],
        compiler_params=_params(("arbitrary",)),
        name="mixer_sample",
    )(u, va, q, k, v, ck, cv, sw, sbb, bias, goa, gob)


def _outproj_kernel(cat_ref, h_ref, w_ref, g_ref, o_ref):
    m = jnp.dot(cat_ref[...], w_ref[...], preferred_element_type=F32)
    o_ref[...] = h_ref[...] + _rms(m, g_ref[...])


def _outproj(cat, h, w_out, g, *, tm):
    m = cat.shape[0]
    return pl.pallas_call(
        _outproj_kernel,
        grid=(m // tm,),
        in_specs=[
            pl.BlockSpec((tm, D_MODEL), lambda i: (i, 0)),
            pl.BlockSpec((tm, D_MODEL), lambda i: (i, 0)),
            pl.BlockSpec((D_MODEL, D_MODEL), lambda i: (0, 0)),
            pl.BlockSpec((1, D_MODEL), lambda i: (0, 0)),
        ],
        out_specs=pl.BlockSpec((tm, D_MODEL), lambda i: (i, 0)),
        out_shape=jax.ShapeDtypeStruct((m, D_MODEL), F32),
        compiler_params=_params(("parallel",)),
        name="outproj",
    )(cat, h, w_out, g)


def _rel_bias_table(rel_bias, lq, lk, band_chunks=None):
    nh = rel_bias.shape[0]
    p = lq + lk - 1
    t_min, t_max = KV_WIN - lk + 1, KV_WIN + lq - 1
    n_lo = max(0, -REL_CLIP - t_min)
    n_hi = max(0, t_max - REL_CLIP)
    mid = rel_bias[:, max(t_min, -REL_CLIP) + REL_CLIP:min(t_max, REL_CLIP) + REL_CLIP + 1]
    f = jnp.concatenate([jnp.broadcast_to(rel_bias[:, :1], (nh, n_lo)), mid,
                         jnp.broadcast_to(rel_bias[:, -1:], (nh, n_hi))], axis=1).astype(F32)
    assert f.shape == (nh, p)
    w = jnp.concatenate([f[:, ::-1], jnp.zeros((nh, 1), F32)], axis=1)
    skew = jnp.broadcast_to(w[:, None, :], (nh, lq, p + 1)).reshape(nh, lq * (p + 1))[:, :lq * p]
    table = skew.reshape(nh, lq, p)[:, :, lq - 1:lq - 1 + lk]
    if band_chunks is not None:
        qc = jnp.arange(lq)[:, None] // CHUNK
        jc = jnp.arange(lk)[None, :] // CHUNK
        table = jnp.where((jc >= qc) & (jc <= qc + band_chunks), table, NEG_INF)
    return table


def kernel(x_prompt, x_sample, cache_b_k, cache_b_v, w_in, w_out, sw_a, sb_a, rel_bias, g_va, g_oa, g_ob,
           g_f1_pre, g_f1_post, g_m_pre, g_m_post, g_f2_pre, g_f2_post,
           f1_gate, f1_up, f1_down, f2_gate, f2_up, f2_down):
    depth = w_in.shape[0]
    assert depth == 1
    bp, sp, _ = x_prompt.shape
    bs, ns, _ = x_sample.shape
    assert bp == 1
    l = 0
    row = lambda g: g[l].reshape(1, -1)
    w_in_b = w_in[l].astype(BF16)
    w_out_b = w_out[l].astype(BF16)
    f1 = (f1_gate[l].astype(BF16), f1_up[l].astype(BF16), f1_down[l].astype(BF16))
    f2 = (f2_gate[l].astype(BF16), f2_up[l].astype(BF16), f2_down[l].astype(BF16))
    sbb = jnp.broadcast_to(sb_a[l][:, :, None], (GROUPS_A, MLP_CHUNK, GROUP_A))
    bias_p = _rel_bias_table(rel_bias[l], GROUP_Q, GROUP_K, band_chunks=LEFT_CHUNKS) * LOG2E
    wc = cache_b_k.shape[2]
    bias_s = _rel_bias_table(rel_bias[l], ns, wc + ns)

    xp = x_prompt.reshape(sp, D_MODEL)
    h = _ffn(xp, row(g_f1_pre), row(g_f1_post), *f1, tm=1024, tf=512)
    u, va, q, k, v, kt, vt = _inproj(h, row(g_m_pre), w_in_b, row(g_va), tm=512,
                                     va_dtype=BF16, kv_dtype=BF16, tail=True)
    cat = _mixer_prompt(u, va, q, k, v, sw_a[l], sbb, bias_p, row(g_oa), row(g_ob), tb=512)
    h2 = _outproj(cat, h, w_out_b, row(g_m_post), tm=512)
    yp = _ffn(h2, row(g_f2_pre), row(g_f2_post), *f2, tm=1024, tf=512)

    ms = bs * ns
    xs = x_sample.reshape(ms, D_MODEL)
    hs = _ffn(xs, row(g_f1_pre), row(g_f1_post), *f1, tm=ms, tf=512)
    us, vas, qs, ks, vs = _inproj(hs, row(g_m_pre), w_in_b, row(g_va), tm=ms,
                                  va_dtype=F32, kv_dtype=F32, tail=False)
    ck = cache_b_k[l].reshape(bs, wc, D_B)
    cv = cache_b_v[l].reshape(bs, wc, D_B)
    cats = _mixer_sample(us, vas, qs, ks, vs, ck, cv, sw_a[l], sbb, bias_s, row(g_oa), row(g_ob), n=ns)
    hs2 = _outproj(cats, hs, w_out_b, row(g_m_post), tm=ms)
    ys = _ffn(hs2, row(g_f2_pre), row(g_f2_post), *f2, tm=ms, tf=512)

    hd = (N_HEADS, HEAD_DIM)
    return (
        yp.reshape(bp, sp, D_MODEL),
        ys.reshape(bs, ns, D_MODEL),
        kt.reshape(depth, bp, KV_WIN, *hd),
        vt.reshape(depth, bp, KV_WIN, *hd),
        ks.reshape(depth, bs, ns, *hd),
        vs.reshape(depth, bs, ns, *hd),
        vas.reshape(depth, bs, ns, D_A),
    )
```

```python
import functools

import jax
import jax.numpy as jnp
from jax import lax
from jax.experimental import pallas as pl
from jax.experimental.pallas import tpu as pltpu

D_MODEL = 2048
D_A = 1024
D_B = 1024
GROUPS_A = 8
GROUP_A = 128
MLP_CHUNK = 128
N_HEADS = 8
HEAD_DIM = 128
CHUNK = 64
LEFT_CHUNKS = 8
KV_WIN = LEFT_CHUNKS * CHUNK
BAND = (LEFT_CHUNKS + 1) * CHUNK
GROUP_CHUNKS = 4
GROUP_Q = GROUP_CHUNKS * CHUNK
GROUP_K = GROUP_Q + KV_WIN
REL_CLIP = 128
D_FF = 5632
D_IN = 2 * D_A + 3 * D_B
EPS = 1e-6
NEG_INF = -1e30
SCALE = HEAD_DIM ** -0.5
LOG2E = 1.4426950408889634

F32 = jnp.float32
BF16 = jnp.bfloat16

VMEM_LIMIT_BYTES = 60 * 1024 * 1024
FFN_ROW_CHUNK = 256
PROJ_ROW_CHUNK = 256


def _rms(x, g):
    return x * lax.rsqrt(jnp.mean(x * x, axis=-1, keepdims=True) + EPS) * g


def _params(sem):
    return pltpu.CompilerParams(dimension_semantics=sem, vmem_limit_bytes=VMEM_LIMIT_BYTES)


def _ffn_kernel(x_ref, gpre_ref, gpost_ref, wg_ref, wu_ref, wd_ref, o_ref, xn_ref, *, rc):
    j = pl.program_id(1)
    nj = pl.num_programs(1)
    tm = x_ref.shape[0]

    def body(first, last):
        for r in range(tm // rc):
            rows = slice(r * rc, (r + 1) * rc)
            if first:
                xn_ref[rows, :] = _rms(x_ref[rows, :], gpre_ref[...]).astype(BF16)
            xn = xn_ref[rows, :]
            g = jnp.dot(xn, wg_ref[...], preferred_element_type=F32)
            u = jnp.dot(xn, wu_ref[...], preferred_element_type=F32)
            a = (g * jax.nn.sigmoid(g) * u).astype(BF16)
            d = jnp.dot(a, wd_ref[...], preferred_element_type=F32)
            if first:
                o_ref[rows, :] = d
            elif last:
                o_ref[rows, :] = x_ref[rows, :] + 0.5 * _rms(o_ref[rows, :] + d, gpost_ref[...])
            else:
                o_ref[rows, :] += d

    pl.when(j == 0)(functools.partial(body, True, False))
    pl.when(jnp.logical_and(j > 0, j < nj - 1))(functools.partial(body, False, False))
    pl.when(j == nj - 1)(functools.partial(body, False, True))


def _ffn(x, gpre, gpost, wg, wu, wd, *, tm, tf):
    m = x.shape[0]
    assert D_FF // tf >= 2
    return pl.pallas_call(
        functools.partial(_ffn_kernel, rc=min(tm, FFN_ROW_CHUNK)),
        grid=(m // tm, D_FF // tf),
        in_specs=[
            pl.BlockSpec((tm, D_MODEL), lambda i, j: (i, 0)),
            pl.BlockSpec((1, D_MODEL), lambda i, j: (0, 0)),
            pl.BlockSpec((1, D_MODEL), lambda i, j: (0, 0)),
            pl.BlockSpec((D_MODEL, tf), lambda i, j: (0, j)),
            pl.BlockSpec((D_MODEL, tf), lambda i, j: (0, j)),
            pl.BlockSpec((tf, D_MODEL), lambda i, j: (j, 0)),
        ],
        out_specs=pl.BlockSpec((tm, D_MODEL), lambda i, j: (i, 0)),
        out_shape=jax.ShapeDtypeStruct((m, D_MODEL), F32),
        scratch_shapes=[pltpu.VMEM((tm, D_MODEL), BF16)],
        compiler_params=_params(("parallel", "arbitrary")),
        name="ffn",
    )(x, gpre, gpost, wg, wu, wd)


def _inproj_kernel(h_ref, gpre_ref, w_ref, gva_ref, u_ref, va_ref, q_ref, k_ref, v_ref, *rest, rc, tail):
    if tail:
        kt_ref, vt_ref, hn_ref = rest
    else:
        (hn_ref,) = rest
    tm = h_ref.shape[0]
    for r in range(tm // rc):
        rows = slice(r * rc, (r + 1) * rc)
        hn_ref[rows, :] = _rms(h_ref[rows, :], gpre_ref[...]).astype(BF16)
        hn = hn_ref[rows, :]
        z = lambda c: jnp.dot(hn, w_ref[:, c * D_A:(c + 1) * D_A], preferred_element_type=F32)
        u_ref[rows, :] = jax.nn.gelu(z(0))
        va_ref[rows, :] = _rms(jax.nn.gelu(z(1)), gva_ref[...]).astype(va_ref.dtype)
        q_ref[rows, :] = z(2).astype(q_ref.dtype)
        for c, o_ref, t_ref in ((3, k_ref, kt_ref if tail else None), (4, v_ref, vt_ref if tail else None)):
            zc = z(c)
            o_ref[rows, :] = zc.astype(o_ref.dtype)
            t0 = r * rc - (tm - KV_WIN)
            if tail and t0 >= 0:
                t_ref[t0:t0 + rc, :] = zc


def _inproj(h, gpre, w_in, gva, *, tm, va_dtype, kv_dtype, tail):
    m = h.shape[0]
    rc = min(tm, PROJ_ROW_CHUNK)
    row_blk = lambda i: (i, 0)
    const = lambda i: (0, 0)
    out_specs = [pl.BlockSpec((tm, D_A), row_blk)] * 5
    out_shape = [
        jax.ShapeDtypeStruct((m, D_A), F32),
        jax.ShapeDtypeStruct((m, D_A), va_dtype),
        jax.ShapeDtypeStruct((m, D_B), BF16),
        jax.ShapeDtypeStruct((m, D_B), kv_dtype),
        jax.ShapeDtypeStruct((m, D_B), kv_dtype),
    ]
    if tail:
        assert tm >= KV_WIN and (tm - KV_WIN) % rc == 0
        out_specs += [pl.BlockSpec((KV_WIN, D_B), const)] * 2
        out_shape += [jax.ShapeDtypeStruct((KV_WIN, D_B), F32)] * 2
    return pl.pallas_call(
        functools.partial(_inproj_kernel, rc=rc, tail=tail),
        grid=(m // tm,),
        in_specs=[
            pl.BlockSpec((tm, D_MODEL), row_blk),
            pl.BlockSpec((1, D_MODEL), const),
            pl.BlockSpec((D_MODEL, D_IN), const, pipeline_mode=pl.Buffered(1)),
            pl.BlockSpec((1, D_A), const),
        ],
        out_specs=out_specs,
        out_shape=out_shape,
        scratch_shapes=[pltpu.VMEM((tm, D_MODEL), BF16)],
        compiler_params=_params(("arbitrary",)),
        name="inproj",
    )(h, gpre, w_in, gva)


def _mask_spatial_weights(sw_ref, wm_ref):
    t = lax.broadcasted_iota(jnp.int32, (MLP_CHUNK, MLP_CHUNK), 0)
    s = lax.broadcasted_iota(jnp.int32, (MLP_CHUNK, MLP_CHUNK), 1)
    for g in range(GROUPS_A):
        wm_ref[g] = jnp.where(s <= t, sw_ref[g], 0.0).astype(BF16)


def _attend(qh, kh, vh, bias, valid):
    s = lax.dot_general(qh, kh, (((1,), (1,)), ((), ())), preferred_element_type=F32) * SCALE + bias
    if valid is not None:
        s = jnp.where(valid, s, NEG_INF)
    e = jnp.exp(s - jnp.max(s, axis=-1, keepdims=True))
    p = e / jnp.sum(e, axis=-1, keepdims=True)
    return jnp.dot(p.astype(BF16), vh, preferred_element_type=F32)


def _gmlp_block(u_ref, va_ref, wm_ref, sbb_ref, goa_ref, cat_ref, a_scr):
    nchunk = u_ref.shape[0] // MLP_CHUNK
    chunk_rows = [slice(n * MLP_CHUNK, (n + 1) * MLP_CHUNK) for n in range(nchunk)]
    for g in range(GROUPS_A):
        cols = slice(g * GROUP_A, (g + 1) * GROUP_A)
        v_all = jnp.concatenate([va_ref[rows, cols] for rows in chunk_rows], axis=1)
        mixed = jnp.dot(wm_ref[g], v_all, preferred_element_type=F32)
        for n, rows in enumerate(chunk_rows):
            a_scr[rows, cols] = u_ref[rows, cols] * (mixed[:, n * GROUP_A:(n + 1) * GROUP_A] + sbb_ref[g])
    for rows in chunk_rows:
        cat_ref[rows, 0:D_A] = _rms(a_scr[rows, :], goa_ref[...]).astype(BF16)


def _attn_groups(q_ref, kbuf, vbuf, bias_ref, gob_ref, cat_ref, o_scr, *, first):
    tb = q_ref.shape[0]
    for gq in range(tb // GROUP_Q):
        q0 = gq * GROUP_Q
        qrows = slice(q0, q0 + GROUP_Q)
        krows = slice(q0, q0 + GROUP_K)
        if first:
            valid = lax.broadcasted_iota(jnp.int32, (GROUP_Q, GROUP_K), 1) >= KV_WIN - q0
        for h in range(N_HEADS):
            cols = slice(h * HEAD_DIM, (h + 1) * HEAD_DIM)
            s2 = lax.dot_general(q_ref[qrows, cols], kbuf[krows, cols], (((1,), (1,)), ((), ())),
                                 preferred_element_type=F32) * (SCALE * LOG2E) + bias_ref[h]
            if first:
                s2 = jnp.where(valid, s2, NEG_INF)
            e = jnp.exp2(s2 - jnp.max(s2, axis=-1, keepdims=True)).astype(BF16)
            nd = jnp.dot(e, vbuf[krows, 2 * h * HEAD_DIM:2 * (h + 1) * HEAD_DIM], preferred_element_type=F32)
            o_scr[:, cols] = nd[:, 0:HEAD_DIM] * (1.0 / nd[:, HEAD_DIM:2 * HEAD_DIM])
        cat_ref[qrows, D_A:D_A + D_B] = _rms(o_scr[...], gob_ref[...]).astype(BF16)


def _mixer_prompt_kernel(u_ref, va_ref, q_ref, k_ref, v_ref, sw_ref, sbb_ref, bias_ref, goa_ref, gob_ref,
                         cat_ref, wm_ref, kbuf, vbuf, a_scr, o_scr):
    i = pl.program_id(0)
    tb = u_ref.shape[0]

    @pl.when(i == 0)
    def _():
        _mask_spatial_weights(sw_ref, wm_ref)
        kbuf[0:KV_WIN, :] = jnp.zeros((KV_WIN, D_B), BF16)
        for h in range(N_HEADS):
            lo = 2 * h * HEAD_DIM
            vbuf[0:KV_WIN, lo:lo + HEAD_DIM] = jnp.zeros((KV_WIN, HEAD_DIM), BF16)
            vbuf[:, lo + HEAD_DIM:lo + 2 * HEAD_DIM] = jnp.ones((KV_WIN + tb, HEAD_DIM), BF16)

    @pl.when(i > 0)
    def _():
        kbuf[0:KV_WIN, :] = kbuf[tb:tb + KV_WIN, :]
        for h in range(N_HEADS):
            lo = 2 * h * HEAD_DIM
            vbuf[0:KV_WIN, lo:lo + HEAD_DIM] = vbuf[tb:tb + KV_WIN, lo:lo + HEAD_DIM]

    kbuf[KV_WIN:KV_WIN + tb, :] = k_ref[...]
    for h in range(N_HEADS):
        vbuf[KV_WIN:KV_WIN + tb, 2 * h * HEAD_DIM:(2 * h + 1) * HEAD_DIM] = v_ref[:, h * HEAD_DIM:(h + 1) * HEAD_DIM]

    _gmlp_block(u_ref, va_ref, wm_ref, sbb_ref, goa_ref, cat_ref, a_scr)

    @pl.when(i == 0)
    def _():
        _attn_groups(q_ref, kbuf, vbuf, bias_ref, gob_ref, cat_ref, o_scr, first=True)

    @pl.when(i > 0)
    def _():
        _attn_groups(q_ref, kbuf, vbuf, bias_ref, gob_ref, cat_ref, o_scr, first=False)


def _mixer_prompt(u, va, q, k, v, sw, sbb, bias, goa, gob, *, tb):
    m = u.shape[0]
    assert tb >= KV_WIN and tb % GROUP_Q == 0
    row_blk = lambda i: (i, 0)
    const2 = lambda i: (0, 0)
    const3 = lambda i: (0, 0, 0)
    return pl.pallas_call(
        _mixer_prompt_kernel,
        grid=(m // tb,),
        in_specs=[
            pl.BlockSpec((tb, D_A), row_blk),
            pl.BlockSpec((tb, D_A), row_blk),
            pl.BlockSpec((tb, D_B), row_blk),
            pl.BlockSpec((tb, D_B), row_blk),
            pl.BlockSpec((tb, D_B), row_blk),
            pl.BlockSpec((GROUPS_A, MLP_CHUNK, MLP_CHUNK), const3),
            pl.BlockSpec((GROUPS_A, MLP_CHUNK, GROUP_A), const3),
            pl.BlockSpec((N_HEADS, GROUP_Q, GROUP_K), const3),
            pl.BlockSpec((1, D_A), const2),
            pl.BlockSpec((1, D_B), const2),
        ],
        out_specs=pl.BlockSpec((tb, D_MODEL), row_blk),
        out_shape=jax.ShapeDtypeStruct((m, D_MODEL), BF16),
        scratch_shapes=[
            pltpu.VMEM((GROUPS_A, MLP_CHUNK, MLP_CHUNK), BF16),
            pltpu.VMEM((KV_WIN + tb, D_B), BF16),
            pltpu.VMEM((KV_WIN + tb, 2 * D_B), BF16),
            pltpu.VMEM((tb, D_A), F32),
            pltpu.VMEM((GROUP_Q, D_B), F32),
        ],
        compiler_params=_params(("arbitrary",)),
        name="mixer_prompt",
    )(u, va, q, k, v, sw, sbb, bias, goa, gob)


def _mixer_sample_kernel(u_ref, va_ref, q_ref, k_ref, v_ref, ck_ref, cv_ref, sw_ref, sbb_ref, bias_ref,
                         goa_ref, gob_ref, cat_ref, wm_ref, vap, kbuf, vbuf, a_scr, o_scr):
    b = pl.program_id(0)
    n = u_ref.shape[0]
    w = ck_ref.shape[1] // N_HEADS

    @pl.when(b == 0)
    def _():
        _mask_spatial_weights(sw_ref, wm_ref)
        vap[...] = jnp.zeros(vap.shape, BF16)

    vap[0:n, :] = va_ref[...].astype(BF16)
    for h in range(N_HEADS):
        cols = slice(h * HEAD_DIM, (h + 1) * HEAD_DIM)
        kbuf[0:w, cols] = ck_ref[0, pl.ds(h, w, stride=N_HEADS), :].astype(BF16)
        vbuf[0:w, cols] = cv_ref[0, pl.ds(h, w, stride=N_HEADS), :].astype(BF16)
    kbuf[w:w + n, :] = k_ref[...].astype(BF16)
    vbuf[w:w + n, :] = v_ref[...].astype(BF16)

    ssq = jnp.zeros((n, 1), F32)
    for g in range(GROUPS_A):
        cols = slice(g * GROUP_A, (g + 1) * GROUP_A)
        mixed = jnp.dot(wm_ref[g, 0:n, :], vap[:, cols], preferred_element_type=F32) + sbb_ref[g, 0:n, :]
        a = u_ref[:, cols] * mixed
        a_scr[:, cols] = a
        ssq = ssq + jnp.sum(a * a, axis=-1, keepdims=True)
    inv = lax.rsqrt(ssq * (1.0 / D_A) + EPS)
    cat_ref[:, 0:D_A] = (a_scr[...] * inv * goa_ref[...]).astype(BF16)

    ssq = jnp.zeros((n, 1), F32)
    for h in range(N_HEADS):
        cols = slice(h * HEAD_DIM, (h + 1) * HEAD_DIM)
        o = _attend(q_ref[:, cols], kbuf[:, cols], vbuf[:, cols], bias_ref[h], None)
        o_scr[:, cols] = o
        ssq = ssq + jnp.sum(o * o, axis=-1, keepdims=True)
    inv = lax.rsqrt(ssq * (1.0 / D_B) + EPS)
    cat_ref[:, D_A:D_A + D_B] = (o_scr[...] * inv * gob_ref[...]).astype(BF16)


def _mixer_sample(u, va, q, k, v, ck, cv, sw, sbb, bias, goa, gob, *, n):
    m = u.shape[0]
    nb, wh, _ = ck.shape
    w = wh // N_HEADS
    row_blk = lambda b: (b, 0)
    const2 = lambda b: (0, 0)
    const3 = lambda b: (0, 0, 0)
    return pl.pallas_call(
        _mixer_sample_kernel,
        grid=(nb,),
        in_specs=[
            pl.BlockSpec((n, D_A), row_blk),
            pl.BlockSpec((n, D_A), row_blk),
            pl.BlockSpec((n, D_B), row_blk),
            pl.BlockSpec((n, D_B), row_blk),
            pl.BlockSpec((n, D_B), row_blk),
            pl.BlockSpec((1, wh, HEAD_DIM), lambda b: (b, 0, 0)),
            pl.BlockSpec((1, wh, HEAD_DIM), lambda b: (b, 0, 0)),
            pl.BlockSpec((GROUPS_A, MLP_CHUNK, MLP_CHUNK), const3),
            pl.BlockSpec((GROUPS_A, MLP_CHUNK, GROUP_A), const3),
            pl.BlockSpec((N_HEADS, n, w + n), const3),
            pl.BlockSpec((1, D_A), const2),
            pl.BlockSpec((1, D_B), const2),
        ],
        out_specs=pl.BlockSpec((n, D_MODEL), row_blk),
        out_shape=jax.ShapeDtypeStruct((m, D_MODEL), BF16),
        scratch_shapes=[
            pltpu.VMEM((GROUPS_A, MLP_CHUNK, MLP_CHUNK), BF16),
            pltpu.VMEM((MLP_CHUNK, D_A), BF16),
            pltpu.VMEM((w + n, D_B), BF16),
            pltpu.VMEM((w + n, D_B), BF16),
            pltpu.VMEM((n, D_A), F32),
            pltpu.VMEM((n, D_B), F32),
        ],
        compiler_params=_params(("arbitrary",)),
        name="mixer_sample",
    )(u, va, q, k, v, ck, cv, sw, sbb, bias, goa, gob)


def _outproj_kernel(cat_ref, h_ref, w_ref, g_ref, o_ref, *, rc):
    for r in range(cat_ref.shape[0] // rc):
        rows = slice(r * rc, (r + 1) * rc)
        m = jnp.dot(cat_ref[rows, :], w_ref[...], preferred_element_type=F32)
        o_ref[rows, :] = h_ref[rows, :] + _rms(m, g_ref[...])


def _outproj(cat, h, w_out, g, *, tm):
    m = cat.shape[0]
    return pl.pallas_call(
        functools.partial(_outproj_kernel, rc=min(tm, PROJ_ROW_CHUNK)),
        grid=(m // tm,),
        in_specs=[
            pl.BlockSpec((tm, D_MODEL), lambda i: (i, 0)),
            pl.BlockSpec((tm, D_MODEL), lambda i: (i, 0)),
            pl.BlockSpec((D_MODEL, D_MODEL), lambda i: (0, 0), pipeline_mode=pl.Buffered(1)),
            pl.BlockSpec((1, D_MODEL), lambda i: (0, 0)),
        ],
        out_specs=pl.BlockSpec((tm, D_MODEL), lambda i: (i, 0)),
        out_shape=jax.ShapeDtypeStruct((m, D_MODEL), F32),
        compiler_params=_params(("parallel",)),
        name="outproj",
    )(cat, h, w_out, g)


def _rel_bias_table(rel_bias, lq, lk, band_chunks=None):
    nh = rel_bias.shape[0]
    p = lq + lk - 1
    t_min, t_max = KV_WIN - lk + 1, KV_WIN + lq - 1
    n_lo = max(0, -REL_CLIP - t_min)
    n_hi = max(0, t_max - REL_CLIP)
    mid = rel_bias[:, max(t_min, -REL_CLIP) + REL_CLIP:min(t_max, REL_CLIP) + REL_CLIP + 1]
    f = jnp.concatenate([jnp.broadcast_to(rel_bias[:, :1], (nh, n_lo)), mid,
                         jnp.broadcast_to(rel_bias[:, -1:], (nh, n_hi))], axis=1).astype(F32)
    assert f.shape == (nh, p)
    w = jnp.concatenate([f[:, ::-1], jnp.zeros((nh, 1), F32)], axis=1)
    skew = jnp.broadcast_to(w[:, None, :], (nh, lq, p + 1)).reshape(nh, lq * (p + 1))[:, :lq * p]
    table = skew.reshape(nh, lq, p)[:, :, lq - 1:lq - 1 + lk]
    if band_chunks is not None:
        qc = jnp.arange(lq)[:, None] // CHUNK
        jc = jnp.arange(lk)[None, :] // CHUNK
        table = jnp.where((jc >= qc) & (jc <= qc + band_chunks), table, NEG_INF)
    return table


def kernel(x_prompt, x_sample, cache_b_k, cache_b_v, w_in, w_out, sw_a, sb_a, rel_bias, g_va, g_oa, g_ob,
           g_f1_pre, g_f1_post, g_m_pre, g_m_post, g_f2_pre, g_f2_post,
           f1_gate, f1_up, f1_down, f2_gate, f2_up, f2_down):
    depth = w_in.shape[0]
    assert depth == 1
    bp, sp, _ = x_prompt.shape
    bs, ns, _ = x_sample.shape
    assert bp == 1
    l = 0
    row = lambda g: g[l].reshape(1, -1)
    w_in_b = w_in[l].astype(BF16)
    w_out_b = w_out[l].astype(BF16)
    f1 = (f1_gate[l].astype(BF16), f1_up[l].astype(BF16), f1_down[l].astype(BF16))
    f2 = (f2_gate[l].astype(BF16), f2_up[l].astype(BF16), f2_down[l].astype(BF16))
    sbb = jnp.broadcast_to(sb_a[l][:, :, None], (GROUPS_A, MLP_CHUNK, GROUP_A))
    bias_p = _rel_bias_table(rel_bias[l], GROUP_Q, GROUP_K, band_chunks=LEFT_CHUNKS) * LOG2E
    wc = cache_b_k.shape[2]
    bias_s = _rel_bias_table(rel_bias[l], ns, wc + ns)

    xp = x_prompt.reshape(sp, D_MODEL)
    h = _ffn(xp, row(g_f1_pre), row(g_f1_post), *f1, tm=1024, tf=512)
    u, va, q, k, v, kt, vt = _inproj(h, row(g_m_pre), w_in_b, row(g_va), tm=512,
                                     va_dtype=BF16, kv_dtype=BF16, tail=True)
    cat = _mixer_prompt(u, va, q, k, v, sw_a[l], sbb, bias_p, row(g_oa), row(g_ob), tb=512)
    h2 = _outproj(cat, h, w_out_b, row(g_m_post), tm=512)
    yp = _ffn(h2, row(g_f2_pre), row(g_f2_post), *f2, tm=1024, tf=512)

    ms = bs * ns
    xs = x_sample.reshape(ms, D_MODEL)
    hs = _ffn(xs, row(g_f1_pre), row(g_f1_post), *f1, tm=ms, tf=512)
    us, vas, qs, ks, vs = _inproj(hs, row(g_m_pre), w_in_b, row(g_va), tm=ms,
                                  va_dtype=F32, kv_dtype=F32, tail=False)
    ck = cache_b_k[l].reshape(bs, wc * N_HEADS, HEAD_DIM)
    cv = cache_b_v[l].reshape(bs, wc * N_HEADS, HEAD_DIM)
    cats = _mixer_sample(us, vas, qs, ks, vs, ck, cv, sw_a[l], sbb, bias_s, row(g_oa), row(g_ob), n=ns)
    hs2 = _outproj(cats, hs, w_out_b, row(g_m_post), tm=ms)
    ys = _ffn(hs2, row(g_f2_pre), row(g_f2_post), *f2, tm=ms, tf=512)

    hd = (N_HEADS, HEAD_DIM)
    return (
        yp.reshape(bp, sp, D_MODEL),
        ys.reshape(bs, ns, D_MODEL),
        kt.reshape(depth, bp, KV_WIN, *hd),
        vt.reshape(depth, bp, KV_WIN, *hd),
        ks.reshape(depth, bs, ns, *hd),
        vs.reshape(depth, bs, ns, *hd),
        vas.reshape(depth, bs, ns, D_A),
    )
```

```python
import functools

import jax
import jax.numpy as jnp
from jax import lax
from jax.experimental import pallas as pl
from jax.experimental.pallas import tpu as pltpu

D_MODEL = 2048
D_A = 1024
D_B = 1024
GROUPS_A = 8
GROUP_A = 128
MLP_CHUNK = 128
N_HEADS = 8
HEAD_DIM = 128
CHUNK = 64
LEFT_CHUNKS = 8
KV_WIN = LEFT_CHUNKS * CHUNK
BAND = (LEFT_CHUNKS + 1) * CHUNK
GROUP_CHUNKS = 4
GROUP_Q = GROUP_CHUNKS * CHUNK
GROUP_K = GROUP_Q + KV_WIN
REL_CLIP = 128
D_FF = 5632
D_IN = 2 * D_A + 3 * D_B
EPS = 1e-6
NEG_INF = -1e30
SCALE = HEAD_DIM ** -0.5
LOG2E = 1.4426950408889634

F32 = jnp.float32
BF16 = jnp.bfloat16

VMEM_LIMIT_BYTES = 60 * 1024 * 1024
FFN_ROW_CHUNK = 256
PROJ_ROW_CHUNK = 256


def _rms(x, g):
    return x * lax.rsqrt(jnp.mean(x * x, axis=-1, keepdims=True) + EPS) * g


def _params(sem):
    return pltpu.CompilerParams(dimension_semantics=sem, vmem_limit_bytes=VMEM_LIMIT_BYTES)


def _ffn_kernel(*refs, rc, convert, resume):
    if convert:
        x_ref, gpre_ref, gpost_ref, wgf_ref, wuf_ref, wdf_ref, o_ref, wg_ref, wu_ref, wd_ref, xn_ref = refs
    elif resume:
        x_ref, gpre_ref, gpost_ref, wg_ref, wu_ref, wd_ref, _, o_ref, xn_ref = refs
    else:
        x_ref, gpre_ref, gpost_ref, wg_ref, wu_ref, wd_ref, o_ref, xn_ref = refs
    j = pl.program_id(1)
    nj = pl.num_programs(1)
    tm = x_ref.shape[0]

    def body(first, last):
        if convert:
            wg_ref[...] = wgf_ref[...].astype(BF16)
            wu_ref[...] = wuf_ref[...].astype(BF16)
            wd_ref[...] = wdf_ref[...].astype(BF16)
        for r in range(tm // rc):
            rows = slice(r * rc, (r + 1) * rc)
            if first:
                xn_ref[rows, :] = _rms(x_ref[rows, :], gpre_ref[...]).astype(BF16)
            xn = xn_ref[rows, :]
            g = jnp.dot(xn, wg_ref[...], preferred_element_type=F32)
            u = jnp.dot(xn, wu_ref[...], preferred_element_type=F32)
            a = (g * jax.nn.sigmoid(g) * u).astype(BF16)
            d = jnp.dot(a, wd_ref[...], preferred_element_type=F32)
            if first:
                o_ref[rows, :] = d
            elif last:
                o_ref[rows, :] = x_ref[rows, :] + 0.5 * _rms(o_ref[rows, :] + d, gpost_ref[...])
            else:
                o_ref[rows, :] += d

    pl.when(j == 0)(functools.partial(body, True, False))
    pl.when(jnp.logical_and(j > 0, j < nj - 1))(functools.partial(body, False, False))
    pl.when(j == nj - 1)(functools.partial(body, False, True))


def _ffn_weight_specs(tf):
    return [
        pl.BlockSpec((D_MODEL, tf), lambda i, j: (0, j)),
        pl.BlockSpec((D_MODEL, tf), lambda i, j: (0, j)),
        pl.BlockSpec((tf, D_MODEL), lambda i, j: (j, 0)),
    ]


def _ffn_head(x, gpre, gpost, wg, wu, wd, *, tm, tf):
    m = x.shape[0]
    assert D_FF // tf >= 2
    vec = pl.BlockSpec((1, D_MODEL), lambda i, j: (0, 0))
    tile0 = lambda **kw: pl.BlockSpec((tm, D_MODEL), lambda i, j: (0, 0), **kw)
    return pl.pallas_call(
        functools.partial(_ffn_kernel, rc=min(tm, FFN_ROW_CHUNK), convert=True, resume=False),
        grid=(1, D_FF // tf),
        in_specs=[tile0(pipeline_mode=pl.Buffered(1)), vec, vec] + _ffn_weight_specs(tf),
        out_specs=[tile0()] + _ffn_weight_specs(tf),
        out_shape=[jax.ShapeDtypeStruct((m, D_MODEL), F32)]
        + [jax.ShapeDtypeStruct(w.shape, BF16) for w in (wg, wu, wd)],
        scratch_shapes=[pltpu.VMEM((tm, D_MODEL), BF16)],
        compiler_params=_params(("arbitrary", "arbitrary")),
        name="ffn_head",
    )(x, gpre, gpost, wg, wu, wd)


def _ffn(x, gpre, gpost, wg, wu, wd, *, tm, tf, resume=None):
    m = x.shape[0]
    assert D_FF // tf >= 2
    skip = 0 if resume is None else 1
    vec = pl.BlockSpec((1, D_MODEL), lambda i, j: (0, 0))
    tile = lambda: pl.BlockSpec((tm, D_MODEL), lambda i, j: (i + skip, 0))
    in_specs = [tile(), vec, vec] + _ffn_weight_specs(tf)
    args = [x, gpre, gpost, wg, wu, wd]
    aliases = {}
    if resume is not None:
        in_specs.append(pl.BlockSpec(memory_space=pl.ANY))
        aliases = {len(args): 0}
        args.append(resume)
    return pl.pallas_call(
        functools.partial(_ffn_kernel, rc=min(tm, FFN_ROW_CHUNK), convert=False, resume=resume is not None),
        grid=(m // tm - skip, D_FF // tf),
        in_specs=in_specs,
        out_specs=tile(),
        out_shape=jax.ShapeDtypeStruct((m, D_MODEL), F32),
        scratch_shapes=[pltpu.VMEM((tm, D_MODEL), BF16)],
        input_output_aliases=aliases,
        compiler_params=_params(("parallel", "arbitrary")),
        name="ffn",
    )(*args)


def _inproj_kernel(h_ref, gpre_ref, w_ref, gva_ref, u_ref, va_ref, q_ref, k_ref, v_ref, *rest, rc, tail):
    if tail:
        kt_ref, vt_ref, hn_ref = rest
    else:
        (hn_ref,) = rest
    tm = h_ref.shape[0]
    for r in range(tm // rc):
        rows = slice(r * rc, (r + 1) * rc)
        hn_ref[rows, :] = _rms(h_ref[rows, :], gpre_ref[...]).astype(BF16)
        hn = hn_ref[rows, :]
        z = lambda c: jnp.dot(hn, w_ref[:, c * D_A:(c + 1) * D_A], preferred_element_type=F32)
        u_ref[rows, :] = jax.nn.gelu(z(0))
        va_ref[rows, :] = _rms(jax.nn.gelu(z(1)), gva_ref[...]).astype(va_ref.dtype)
        q_ref[rows, :] = z(2).astype(q_ref.dtype)
        for c, o_ref, t_ref in ((3, k_ref, kt_ref if tail else None), (4, v_ref, vt_ref if tail else None)):
            zc = z(c)
            o_ref[rows, :] = zc.astype(o_ref.dtype)
            t0 = r * rc - (tm - KV_WIN)
            if tail and t0 >= 0:
                t_ref[t0:t0 + rc, :] = zc


def _inproj(h, gpre, w_in, gva, *, tm, va_dtype, kv_dtype, tail):
    m = h.shape[0]
    rc = min(tm, PROJ_ROW_CHUNK)
    row_blk = lambda i: (i, 0)
    const = lambda i: (0, 0)
    out_specs = [pl.BlockSpec((tm, D_A), row_blk)] * 5
    out_shape = [
        jax.ShapeDtypeStruct((m, D_A), F32),
        jax.ShapeDtypeStruct((m, D_A), va_dtype),
        jax.ShapeDtypeStruct((m, D_B), BF16),
        jax.ShapeDtypeStruct((m, D_B), kv_dtype),
        jax.ShapeDtypeStruct((m, D_B), kv_dtype),
    ]
    if tail:
        assert tm >= KV_WIN and (tm - KV_WIN) % rc == 0
        out_specs += [pl.BlockSpec((KV_WIN, D_B), const)] * 2
        out_shape += [jax.ShapeDtypeStruct((KV_WIN, D_B), F32)] * 2
    return pl.pallas_call(
        functools.partial(_inproj_kernel, rc=rc, tail=tail),
        grid=(m // tm,),
        in_specs=[
            pl.BlockSpec((tm, D_MODEL), row_blk),
            pl.BlockSpec((1, D_MODEL), const),
            pl.BlockSpec((D_MODEL, D_IN), const, pipeline_mode=pl.Buffered(1)),
            pl.BlockSpec((1, D_A), const),
        ],
        out_specs=out_specs,
        out_shape=out_shape,
        scratch_shapes=[pltpu.VMEM((tm, D_MODEL), BF16)],
        compiler_params=_params(("arbitrary",)),
        name="inproj",
    )(h, gpre, w_in, gva)


def _mask_spatial_weights(sw_ref, wm_ref):
    t = lax.broadcasted_iota(jnp.int32, (MLP_CHUNK, MLP_CHUNK), 0)
    s = lax.broadcasted_iota(jnp.int32, (MLP_CHUNK, MLP_CHUNK), 1)
    for g in range(GROUPS_A):
        wm_ref[g] = jnp.where(s <= t, sw_ref[g], 0.0).astype(BF16)


def _attend(qh, kh, vh, bias, valid):
    s = lax.dot_general(qh, kh, (((1,), (1,)), ((), ())), preferred_element_type=F32) * SCALE + bias
    if valid is not None:
        s = jnp.where(valid, s, NEG_INF)
    e = jnp.exp(s - jnp.max(s, axis=-1, keepdims=True))
    p = e / jnp.sum(e, axis=-1, keepdims=True)
    return jnp.dot(p.astype(BF16), vh, preferred_element_type=F32)


def _gmlp_block(u_ref, va_ref, wm_ref, sbb_ref, goa_ref, cat_ref, a_scr):
    nchunk = u_ref.shape[0] // MLP_CHUNK
    chunk_rows = [slice(n * MLP_CHUNK, (n + 1) * MLP_CHUNK) for n in range(nchunk)]
    for g in range(GROUPS_A):
        cols = slice(g * GROUP_A, (g + 1) * GROUP_A)
        v_all = jnp.concatenate([va_ref[rows, cols] for rows in chunk_rows], axis=1)
        mixed = jnp.dot(wm_ref[g], v_all, preferred_element_type=F32)
        for n, rows in enumerate(chunk_rows):
            a_scr[rows, cols] = u_ref[rows, cols] * (mixed[:, n * GROUP_A:(n + 1) * GROUP_A] + sbb_ref[g])
    for rows in chunk_rows:
        cat_ref[rows, 0:D_A] = _rms(a_scr[rows, :], goa_ref[...]).astype(BF16)


def _attn_groups(q_ref, kbuf, vbuf, bias_ref, gob_ref, cat_ref, o_scr, *, first):
    tb = q_ref.shape[0]
    for gq in range(tb // GROUP_Q):
        q0 = gq * GROUP_Q
        qrows = slice(q0, q0 + GROUP_Q)
        krows = slice(q0, q0 + GROUP_K)
        if first:
            valid = lax.broadcasted_iota(jnp.int32, (GROUP_Q, GROUP_K), 1) >= KV_WIN - q0
        for h in range(N_HEADS):
            cols = slice(h * HEAD_DIM, (h + 1) * HEAD_DIM)
            s2 = lax.dot_general(q_ref[qrows, cols], kbuf[krows, cols], (((1,), (1,)), ((), ())),
                                 preferred_element_type=F32) * (SCALE * LOG2E) + bias_ref[h]
            if first:
                s2 = jnp.where(valid, s2, NEG_INF)
            e = jnp.exp2(s2 - jnp.max(s2, axis=-1, keepdims=True)).astype(BF16)
            nd = jnp.dot(e, vbuf[krows, 2 * h * HEAD_DIM:2 * (h + 1) * HEAD_DIM], preferred_element_type=F32)
            o_scr[:, cols] = nd[:, 0:HEAD_DIM] * (1.0 / nd[:, HEAD_DIM:2 * HEAD_DIM])
        cat_ref[qrows, D_A:D_A + D_B] = _rms(o_scr[...], gob_ref[...]).astype(BF16)


def _mixer_prompt_kernel(u_ref, va_ref, q_ref, k_ref, v_ref, sw_ref, sbb_ref, bias_ref, goa_ref, gob_ref,
                         cat_ref, wm_ref, kbuf, vbuf, a_scr, o_scr):
    i = pl.program_id(0)
    tb = u_ref.shape[0]

    @pl.when(i == 0)
    def _():
        _mask_spatial_weights(sw_ref, wm_ref)
        kbuf[0:KV_WIN, :] = jnp.zeros((KV_WIN, D_B), BF16)
        for h in range(N_HEADS):
            lo = 2 * h * HEAD_DIM
            vbuf[0:KV_WIN, lo:lo + HEAD_DIM] = jnp.zeros((KV_WIN, HEAD_DIM), BF16)
            vbuf[:, lo + HEAD_DIM:lo + 2 * HEAD_DIM] = jnp.ones((KV_WIN + tb, HEAD_DIM), BF16)

    @pl.when(i > 0)
    def _():
        kbuf[0:KV_WIN, :] = kbuf[tb:tb + KV_WIN, :]
        for h in range(N_HEADS):
            lo = 2 * h * HEAD_DIM
            vbuf[0:KV_WIN, lo:lo + HEAD_DIM] = vbuf[tb:tb + KV_WIN, lo:lo + HEAD_DIM]

    kbuf[KV_WIN:KV_WIN + tb, :] = k_ref[...]
    for h in range(N_HEADS):
        vbuf[KV_WIN:KV_WIN + tb, 2 * h * HEAD_DIM:(2 * h + 1) * HEAD_DIM] = v_ref[:, h * HEAD_DIM:(h + 1) * HEAD_DIM]

    _gmlp_block(u_ref, va_ref, wm_ref, sbb_ref, goa_ref, cat_ref, a_scr)

    @pl.when(i == 0)
    def _():
        _attn_groups(q_ref, kbuf, vbuf, bias_ref, gob_ref, cat_ref, o_scr, first=True)

    @pl.when(i > 0)
    def _():
        _attn_groups(q_ref, kbuf, vbuf, bias_ref, gob_ref, cat_ref, o_scr, first=False)


def _mixer_prompt(u, va, q, k, v, sw, sbb, bias, goa, gob, *, tb):
    m = u.shape[0]
    assert tb >= KV_WIN and tb % GROUP_Q == 0
    row_blk = lambda i: (i, 0)
    const2 = lambda i: (0, 0)
    const3 = lambda i: (0, 0, 0)
    return pl.pallas_call(
        _mixer_prompt_kernel,
        grid=(m // tb,),
        in_specs=[
            pl.BlockSpec((tb, D_A), row_blk),
            pl.BlockSpec((tb, D_A), row_blk),
            pl.BlockSpec((tb, D_B), row_blk),
            pl.BlockSpec((tb, D_B), row_blk),
            pl.BlockSpec((tb, D_B), row_blk),
            pl.BlockSpec((GROUPS_A, MLP_CHUNK, MLP_CHUNK), const3),
            pl.BlockSpec((GROUPS_A, MLP_CHUNK, GROUP_A), const3),
            pl.BlockSpec((N_HEADS, GROUP_Q, GROUP_K), const3),
            pl.BlockSpec((1, D_A), const2),
            pl.BlockSpec((1, D_B), const2),
        ],
        out_specs=pl.BlockSpec((tb, D_MODEL), row_blk),
        out_shape=jax.ShapeDtypeStruct((m, D_MODEL), BF16),
        scratch_shapes=[
            pltpu.VMEM((GROUPS_A, MLP_CHUNK, MLP_CHUNK), BF16),
            pltpu.VMEM((KV_WIN + tb, D_B), BF16),
            pltpu.VMEM((KV_WIN + tb, 2 * D_B), BF16),
            pltpu.VMEM((tb, D_A), F32),
            pltpu.VMEM((GROUP_Q, D_B), F32),
        ],
        compiler_params=_params(("arbitrary",)),
        name="mixer_prompt",
    )(u, va, q, k, v, sw, sbb, bias, goa, gob)


def _mixer_sample_kernel(u_ref, va_ref, q_ref, k_ref, v_ref, ck_ref, cv_ref, sw_ref, sbb_ref, bias_ref,
                         goa_ref, gob_ref, cat_ref, wm_ref, vap, kbuf, vbuf, a_scr, o_scr):
    b = pl.program_id(0)
    n = u_ref.shape[0]
    w = ck_ref.shape[1] // N_HEADS

    @pl.when(b == 0)
    def _():
        _mask_spatial_weights(sw_ref, wm_ref)
        vap[...] = jnp.zeros(vap.shape, BF16)

    vap[0:n, :] = va_ref[...].astype(BF16)
    for h in range(N_HEADS):
        cols = slice(h * HEAD_DIM, (h + 1) * HEAD_DIM)
        kbuf[0:w, cols] = ck_ref[0, pl.ds(h, w, stride=N_HEADS), :].astype(BF16)
        vbuf[0:w, cols] = cv_ref[0, pl.ds(h, w, stride=N_HEADS), :].astype(BF16)
    kbuf[w:w + n, :] = k_ref[...].astype(BF16)
    vbuf[w:w + n, :] = v_ref[...].astype(BF16)

    ssq = jnp.zeros((n, 1), F32)
    for g in range(GROUPS_A):
        cols = slice(g * GROUP_A, (g + 1) * GROUP_A)
        mixed = jnp.dot(wm_ref[g, 0:n, :], vap[:, cols], preferred_element_type=F32) + sbb_ref[g, 0:n, :]
        a = u_ref[:, cols] * mixed
        a_scr[:, cols] = a
        ssq = ssq + jnp.sum(a * a, axis=-1, keepdims=True)
    inv = lax.rsqrt(ssq * (1.0 / D_A) + EPS)
    cat_ref[:, 0:D_A] = (a_scr[...] * inv * goa_ref[...]).astype(BF16)

    ssq = jnp.zeros((n, 1), F32)
    for h in range(N_HEADS):
        cols = slice(h * HEAD_DIM, (h + 1) * HEAD_DIM)
        o = _attend(q_ref[:, cols], kbuf[:, cols], vbuf[:, cols], bias_ref[h], None)
        o_scr[:, cols] = o
        ssq = ssq + jnp.sum(o * o, axis=-1, keepdims=True)
    inv = lax.rsqrt(ssq * (1.0 / D_B) + EPS)
    cat_ref[:, D_A:D_A + D_B] = (o_scr[...] * inv * gob_ref[...]).astype(BF16)


def _mixer_sample(u, va, q, k, v, ck, cv, sw, sbb, bias, goa, gob, *, n):
    m = u.shape[0]
    nb, wh, _ = ck.shape
    w = wh // N_HEADS
    row_blk = lambda b: (b, 0)
    const2 = lambda b: (0, 0)
    const3 = lambda b: (0, 0, 0)
    return pl.pallas_call(
        _mixer_sample_kernel,
        grid=(nb,),
        in_specs=[
            pl.BlockSpec((n, D_A), row_blk),
            pl.BlockSpec((n, D_A), row_blk),
            pl.BlockSpec((n, D_B), row_blk),
            pl.BlockSpec((n, D_B), row_blk),
            pl.BlockSpec((n, D_B), row_blk),
            pl.BlockSpec((1, wh, HEAD_DIM), lambda b: (b, 0, 0)),
            pl.BlockSpec((1, wh, HEAD_DIM), lambda b: (b, 0, 0)),
            pl.BlockSpec((GROUPS_A, MLP_CHUNK, MLP_CHUNK), const3),
            pl.BlockSpec((GROUPS_A, MLP_CHUNK, GROUP_A), const3),
            pl.BlockSpec((N_HEADS, n, w + n), const3),
            pl.BlockSpec((1, D_A), const2),
            pl.BlockSpec((1, D_B), const2),
        ],
        out_specs=pl.BlockSpec((n, D_MODEL), row_blk),
        out_shape=jax.ShapeDtypeStruct((m, D_MODEL), BF16),
        scratch_shapes=[
            pltpu.VMEM((GROUPS_A, MLP_CHUNK, MLP_CHUNK), BF16),
            pltpu.VMEM((MLP_CHUNK, D_A), BF16),
            pltpu.VMEM((w + n, D_B), BF16),
            pltpu.VMEM((w + n, D_B), BF16),
            pltpu.VMEM((n, D_A), F32),
            pltpu.VMEM((n, D_B), F32),
        ],
        compiler_params=_params(("arbitrary",)),
        name="mixer_sample",
    )(u, va, q, k, v, ck, cv, sw, sbb, bias, goa, gob)


def _outproj_kernel(cat_ref, h_ref, w_ref, g_ref, o_ref, *, rc):
    for r in range(cat_ref.shape[0] // rc):
        rows = slice(r * rc, (r + 1) * rc)
        m = jnp.dot(cat_ref[rows, :], w_ref[...], preferred_element_type=F32)
        o_ref[rows, :] = h_ref[rows, :] + _rms(m, g_ref[...])


def _outproj(cat, h, w_out, g, *, tm):
    m = cat.shape[0]
    return pl.pallas_call(
        functools.partial(_outproj_kernel, rc=min(tm, PROJ_ROW_CHUNK)),
        grid=(m // tm,),
        in_specs=[
            pl.BlockSpec((tm, D_MODEL), lambda i: (i, 0)),
            pl.BlockSpec((tm, D_MODEL), lambda i: (i, 0)),
            pl.BlockSpec((D_MODEL, D_MODEL), lambda i: (0, 0), pipeline_mode=pl.Buffered(1)),
            pl.BlockSpec((1, D_MODEL), lambda i: (0, 0)),
        ],
        out_specs=pl.BlockSpec((tm, D_MODEL), lambda i: (i, 0)),
        out_shape=jax.ShapeDtypeStruct((m, D_MODEL), F32),
        compiler_params=_params(("parallel",)),
        name="outproj",
    )(cat, h, w_out, g)


def _rel_bias_table(rel_bias, lq, lk, band_chunks=None):
    nh = rel_bias.shape[0]
    p = lq + lk - 1
    t_min, t_max = KV_WIN - lk + 1, KV_WIN + lq - 1
    n_lo = max(0, -REL_CLIP - t_min)
    n_hi = max(0, t_max - REL_CLIP)
    mid = rel_bias[:, max(t_min, -REL_CLIP) + REL_CLIP:min(t_max, REL_CLIP) + REL_CLIP + 1]
    f = jnp.concatenate([jnp.broadcast_to(rel_bias[:, :1], (nh, n_lo)), mid,
                         jnp.broadcast_to(rel_bias[:, -1:], (nh, n_hi))], axis=1).astype(F32)
    assert f.shape == (nh, p)
    w = jnp.concatenate([f[:, ::-1], jnp.zeros((nh, 1), F32)], axis=1)
    skew = jnp.broadcast_to(w[:, None, :], (nh, lq, p + 1)).reshape(nh, lq * (p + 1))[:, :lq * p]
    table = skew.reshape(nh, lq, p)[:, :, lq - 1:lq - 1 + lk]
    if band_chunks is not None:
        qc = jnp.arange(lq)[:, None] // CHUNK
        jc = jnp.arange(lk)[None, :] // CHUNK
        table = jnp.where((jc >= qc) & (jc <= qc + band_chunks), table, NEG_INF)
    return table


def kernel(x_prompt, x_sample, cache_b_k, cache_b_v, w_in, w_out, sw_a, sb_a, rel_bias, g_va, g_oa, g_ob,
           g_f1_pre, g_f1_post, g_m_pre, g_m_post, g_f2_pre, g_f2_post,
           f1_gate, f1_up, f1_down, f2_gate, f2_up, f2_down):
    depth = w_in.shape[0]
    assert depth == 1
    bp, sp, _ = x_prompt.shape
    bs, ns, _ = x_sample.shape
    assert bp == 1
    l = 0
    row = lambda g: g[l].reshape(1, -1)
    w_in_b = w_in[l].astype(BF16)
    w_out_b = w_out[l].astype(BF16)
    sbb = jnp.broadcast_to(sb_a[l][:, :, None], (GROUPS_A, MLP_CHUNK, GROUP_A))
    bias_p = _rel_bias_table(rel_bias[l], GROUP_Q, GROUP_K, band_chunks=LEFT_CHUNKS) * LOG2E
    wc = cache_b_k.shape[2]
    bias_s = _rel_bias_table(rel_bias[l], ns, wc + ns)

    xp = x_prompt.reshape(sp, D_MODEL)
    h_head, *f1 = _ffn_head(xp, row(g_f1_pre), row(g_f1_post), f1_gate[l], f1_up[l], f1_down[l],
                            tm=1024, tf=256)
    h = _ffn(xp, row(g_f1_pre), row(g_f1_post), *f1, tm=1024, tf=512, resume=h_head)
    u, va, q, k, v, kt, vt = _inproj(h, row(g_m_pre), w_in_b, row(g_va), tm=512,
                                     va_dtype=BF16, kv_dtype=BF16, tail=True)
    cat = _mixer_prompt(u, va, q, k, v, sw_a[l], sbb, bias_p, row(g_oa), row(g_ob), tb=512)
    h2 = _outproj(cat, h, w_out_b, row(g_m_post), tm=512)
    y_head, *f2 = _ffn_head(h2, row(g_f2_pre), row(g_f2_post), f2_gate[l], f2_up[l], f2_down[l],
                            tm=1024, tf=256)
    yp = _ffn(h2, row(g_f2_pre), row(g_f2_post), *f2, tm=1024, tf=512, resume=y_head)

    ms = bs * ns
    xs = x_sample.reshape(ms, D_MODEL)
    hs = _ffn(xs, row(g_f1_pre), row(g_f1_post), *f1, tm=ms, tf=512)
    us, vas, qs, ks, vs = _inproj(hs, row(g_m_pre), w_in_b, row(g_va), tm=ms,
                                  va_dtype=F32, kv_dtype=F32, tail=False)
    ck = cache_b_k[l].reshape(bs, wc * N_HEADS, HEAD_DIM)
    cv = cache_b_v[l].reshape(bs, wc * N_HEADS, HEAD_DIM)
    cats = _mixer_sample(us, vas, qs, ks, vs, ck, cv, sw_a[l], sbb, bias_s, row(g_oa), row(g_ob), n=ns)
    hs2 = _outproj(cats, hs, w_out_b, row(g_m_post), tm=ms)
    ys = _ffn(hs2, row(g_f2_pre), row(g_f2_post), *f2, tm=ms, tf=512)

    hd = (N_HEADS, HEAD_DIM)
    return (
        yp.reshape(bp, sp, D_MODEL),
        ys.reshape(bs, ns, D_MODEL),
        kt.reshape(depth, bp, KV_WIN, *hd),
        vt.reshape(depth, bp, KV_WIN, *hd),
        ks.reshape(depth, bs, ns, *hd),
        vs.reshape(depth, bs, ns, *hd),
        vas.reshape(depth, bs, ns, D_A),
    )
```

```python
import functools

import jax
import jax.numpy as jnp
from jax import lax
from jax.experimental import pallas as pl
from jax.experimental.pallas import tpu as pltpu

D_MODEL = 2048
D_A = 1024
D_B = 1024
GROUPS_A = 8
GROUP_A = 128
MLP_CHUNK = 128
N_HEADS = 8
HEAD_DIM = 128
CHUNK = 64
LEFT_CHUNKS = 8
KV_WIN = LEFT_CHUNKS * CHUNK
BAND = (LEFT_CHUNKS + 1) * CHUNK
GROUP_CHUNKS = 4
GROUP_Q = GROUP_CHUNKS * CHUNK
GROUP_K = GROUP_Q + KV_WIN
REL_CLIP = 128
D_FF = 5632
D_IN = 2 * D_A + 3 * D_B
EPS = 1e-6
NEG_INF = -1e30
SCALE = HEAD_DIM ** -0.5
LOG2E = 1.4426950408889634

F32 = jnp.float32
BF16 = jnp.bfloat16

VMEM_LIMIT_BYTES = 60 * 1024 * 1024
FFN_ROW_CHUNK = 256
PROJ_ROW_CHUNK = 256


def _rms(x, g):
    return x * lax.rsqrt(jnp.mean(x * x, axis=-1, keepdims=True) + EPS) * g


def _params(sem):
    return pltpu.CompilerParams(dimension_semantics=sem, vmem_limit_bytes=VMEM_LIMIT_BYTES)


def _ffn_kernel(*refs, rc, convert, resume):
    if convert:
        x_ref, gpre_ref, gpost_ref, wgf_ref, wuf_ref, wdf_ref, o_ref, wg_ref, wu_ref, wd_ref, xn_ref = refs
    elif resume:
        x_ref, gpre_ref, gpost_ref, wg_ref, wu_ref, wd_ref, _, o_ref, xn_ref = refs
    else:
        x_ref, gpre_ref, gpost_ref, wg_ref, wu_ref, wd_ref, o_ref, xn_ref = refs
    j = pl.program_id(1)
    nj = pl.num_programs(1)
    tm = x_ref.shape[0]

    def body(first, last):
        if convert:
            wg_ref[...] = wgf_ref[...].astype(BF16)
            wu_ref[...] = wuf_ref[...].astype(BF16)
            wd_ref[...] = wdf_ref[...].astype(BF16)
        for r in range(tm // rc):
            rows = slice(r * rc, (r + 1) * rc)
            if first:
                xn_ref[rows, :] = _rms(x_ref[rows, :], gpre_ref[...]).astype(BF16)
            xn = xn_ref[rows, :]
            g = jnp.dot(xn, wg_ref[...], preferred_element_type=F32)
            u = jnp.dot(xn, wu_ref[...], preferred_element_type=F32)
            a = (g * jax.nn.sigmoid(g) * u).astype(BF16)
            d = jnp.dot(a, wd_ref[...], preferred_element_type=F32)
            if first:
                o_ref[rows, :] = d
            elif last:
                o_ref[rows, :] = x_ref[rows, :] + 0.5 * _rms(o_ref[rows, :] + d, gpost_ref[...])
            else:
                o_ref[rows, :] += d

    pl.when(j == 0)(functools.partial(body, True, False))
    pl.when(jnp.logical_and(j > 0, j < nj - 1))(functools.partial(body, False, False))
    pl.when(j == nj - 1)(functools.partial(body, False, True))


def _ffn_weight_specs(tf):
    return [
        pl.BlockSpec((D_MODEL, tf), lambda i, j: (0, j)),
        pl.BlockSpec((D_MODEL, tf), lambda i, j: (0, j)),
        pl.BlockSpec((tf, D_MODEL), lambda i, j: (j, 0)),
    ]


def _ffn_head(x, gpre, gpost, wg, wu, wd, *, tm, tf):
    m = x.shape[0]
    assert D_FF // tf >= 2
    vec = pl.BlockSpec((1, D_MODEL), lambda i, j: (0, 0))
    tile0 = lambda **kw: pl.BlockSpec((tm, D_MODEL), lambda i, j: (0, 0), **kw)
    return pl.pallas_call(
        functools.partial(_ffn_kernel, rc=min(tm, FFN_ROW_CHUNK), convert=True, resume=False),
        grid=(1, D_FF // tf),
        in_specs=[tile0(pipeline_mode=pl.Buffered(1)), vec, vec] + _ffn_weight_specs(tf),
        out_specs=[tile0()] + _ffn_weight_specs(tf),
        out_shape=[jax.ShapeDtypeStruct((m, D_MODEL), F32)]
        + [jax.ShapeDtypeStruct(w.shape, BF16) for w in (wg, wu, wd)],
        scratch_shapes=[pltpu.VMEM((tm, D_MODEL), BF16)],
        compiler_params=_params(("arbitrary", "arbitrary")),
        name="ffn_head",
    )(x, gpre, gpost, wg, wu, wd)


def _ffn(x, gpre, gpost, wg, wu, wd, *, tm, tf, resume=None):
    m = x.shape[0]
    assert D_FF // tf >= 2
    skip = 0 if resume is None else 1
    vec = pl.BlockSpec((1, D_MODEL), lambda i, j: (0, 0))
    tile = lambda: pl.BlockSpec((tm, D_MODEL), lambda i, j: (i + skip, 0))
    in_specs = [tile(), vec, vec] + _ffn_weight_specs(tf)
    args = [x, gpre, gpost, wg, wu, wd]
    aliases = {}
    if resume is not None:
        in_specs.append(pl.BlockSpec(memory_space=pl.ANY))
        aliases = {len(args): 0}
        args.append(resume)
    return pl.pallas_call(
        functools.partial(_ffn_kernel, rc=min(tm, FFN_ROW_CHUNK), convert=False, resume=resume is not None),
        grid=(m // tm - skip, D_FF // tf),
        in_specs=in_specs,
        out_specs=tile(),
        out_shape=jax.ShapeDtypeStruct((m, D_MODEL), F32),
        scratch_shapes=[pltpu.VMEM((tm, D_MODEL), BF16)],
        input_output_aliases=aliases,
        compiler_params=_params(("parallel", "arbitrary")),
        name="ffn",
    )(*args)


def _inproj_kernel(h_ref, gpre_ref, w_ref, gva_ref, u_ref, va_ref, q_ref, k_ref, v_ref, *rest, rc, tail):
    if tail:
        kt_ref, vt_ref, hn_ref = rest
    else:
        (hn_ref,) = rest
    tm = h_ref.shape[0]
    for r in range(tm // rc):
        rows = slice(r * rc, (r + 1) * rc)
        hn_ref[rows, :] = _rms(h_ref[rows, :], gpre_ref[...]).astype(BF16)
        hn = hn_ref[rows, :]
        z = lambda c: jnp.dot(hn, w_ref[:, c * D_A:(c + 1) * D_A], preferred_element_type=F32)
        u_ref[rows, :] = jax.nn.gelu(z(0))
        va_ref[rows, :] = _rms(jax.nn.gelu(z(1)), gva_ref[...]).astype(va_ref.dtype)
        q_ref[rows, :] = z(2).astype(q_ref.dtype)
        for c, o_ref, t_ref in ((3, k_ref, kt_ref if tail else None), (4, v_ref, vt_ref if tail else None)):
            zc = z(c)
            o_ref[rows, :] = zc.astype(o_ref.dtype)
            t0 = r * rc - (tm - KV_WIN)
            if tail and t0 >= 0:
                t_ref[t0:t0 + rc, :] = zc


def _inproj(h, gpre, w_in, gva, *, tm, va_dtype, kv_dtype, tail):
    m = h.shape[0]
    rc = min(tm, PROJ_ROW_CHUNK)
    row_blk = lambda i: (i, 0)
    const = lambda i: (0, 0)
    out_specs = [pl.BlockSpec((tm, D_A), row_blk)] * 5
    out_shape = [
        jax.ShapeDtypeStruct((m, D_A), F32),
        jax.ShapeDtypeStruct((m, D_A), va_dtype),
        jax.ShapeDtypeStruct((m, D_B), BF16),
        jax.ShapeDtypeStruct((m, D_B), kv_dtype),
        jax.ShapeDtypeStruct((m, D_B), kv_dtype),
    ]
    if tail:
        assert tm >= KV_WIN and (tm - KV_WIN) % rc == 0
        out_specs += [pl.BlockSpec((KV_WIN, D_B), const)] * 2
        out_shape += [jax.ShapeDtypeStruct((KV_WIN, D_B), F32)] * 2
    return pl.pallas_call(
        functools.partial(_inproj_kernel, rc=rc, tail=tail),
        grid=(m // tm,),
        in_specs=[
            pl.BlockSpec((tm, D_MODEL), row_blk),
            pl.BlockSpec((1, D_MODEL), const),
            pl.BlockSpec((D_MODEL, D_IN), const, pipeline_mode=pl.Buffered(1)),
            pl.BlockSpec((1, D_A), const),
        ],
        out_specs=out_specs,
        out_shape=out_shape,
        scratch_shapes=[pltpu.VMEM((tm, D_MODEL), BF16)],
        compiler_params=_params(("arbitrary",)),
        name="inproj",
    )(h, gpre, w_in, gva)


def _mask_spatial_weights(sw_ref, wm_ref):
    t = lax.broadcasted_iota(jnp.int32, (MLP_CHUNK, MLP_CHUNK), 0)
    s = lax.broadcasted_iota(jnp.int32, (MLP_CHUNK, MLP_CHUNK), 1)
    for g in range(GROUPS_A):
        wm_ref[g] = jnp.where(s <= t, sw_ref[g], 0.0).astype(BF16)


def _attend(qh, kh, vh, bias, valid):
    s = lax.dot_general(qh, kh, (((1,), (1,)), ((), ())), preferred_element_type=F32) * SCALE + bias
    if valid is not None:
        s = jnp.where(valid, s, NEG_INF)
    e = jnp.exp(s - jnp.max(s, axis=-1, keepdims=True))
    p = e / jnp.sum(e, axis=-1, keepdims=True)
    return jnp.dot(p.astype(BF16), vh, preferred_element_type=F32)


def _gmlp_block(u_ref, va_ref, wm_ref, sbb_ref, goa_ref, cat_ref, a_scr):
    nchunk = u_ref.shape[0] // MLP_CHUNK
    chunk_rows = [slice(n * MLP_CHUNK, (n + 1) * MLP_CHUNK) for n in range(nchunk)]
    for g in range(GROUPS_A):
        cols = slice(g * GROUP_A, (g + 1) * GROUP_A)
        v_all = jnp.concatenate([va_ref[rows, cols] for rows in chunk_rows], axis=1)
        mixed = jnp.dot(wm_ref[g], v_all, preferred_element_type=F32)
        for n, rows in enumerate(chunk_rows):
            a_scr[rows, cols] = u_ref[rows, cols] * (mixed[:, n * GROUP_A:(n + 1) * GROUP_A] + sbb_ref[g])
    for rows in chunk_rows:
        cat_ref[rows, 0:D_A] = _rms(a_scr[rows, :], goa_ref[...]).astype(BF16)


def _attn_groups(q_ref, kbuf, vbuf, bias_ref, gob_ref, cat_ref, o_scr, *, first):
    tb = q_ref.shape[0]
    for gq in range(tb // GROUP_Q):
        q0 = gq * GROUP_Q
        qrows = slice(q0, q0 + GROUP_Q)
        krows = slice(q0, q0 + GROUP_K)
        if first:
            valid = lax.broadcasted_iota(jnp.int32, (GROUP_Q, GROUP_K), 1) >= KV_WIN - q0
        for h in range(N_HEADS):
            cols = slice(h * HEAD_DIM, (h + 1) * HEAD_DIM)
            s2 = lax.dot_general(q_ref[qrows, cols], kbuf[krows, cols], (((1,), (1,)), ((), ())),
                                 preferred_element_type=F32) * (SCALE * LOG2E) + bias_ref[h]
            if first:
                s2 = jnp.where(valid, s2, NEG_INF)
            e = jnp.exp2(s2 - jnp.max(s2, axis=-1, keepdims=True)).astype(BF16)
            nd = jnp.dot(e, vbuf[krows, 2 * h * HEAD_DIM:2 * (h + 1) * HEAD_DIM], preferred_element_type=F32)
            o_scr[:, cols] = nd[:, 0:HEAD_DIM] * (1.0 / nd[:, HEAD_DIM:2 * HEAD_DIM])
        cat_ref[qrows, D_A:D_A + D_B] = _rms(o_scr[...], gob_ref[...]).astype(BF16)


def _mixer_prompt_kernel(u_ref, va_ref, q_ref, k_ref, v_ref, sw_ref, sbb_ref, bias_ref, goa_ref, gob_ref,
                         cat_ref, wm_ref, kbuf, vbuf, a_scr, o_scr):
    i = pl.program_id(0)
    tb = u_ref.shape[0]

    @pl.when(i == 0)
    def _():
        _mask_spatial_weights(sw_ref, wm_ref)
        kbuf[0:KV_WIN, :] = jnp.zeros((KV_WIN, D_B), BF16)
        for h in range(N_HEADS):
            lo = 2 * h * HEAD_DIM
            vbuf[0:KV_WIN, lo:lo + HEAD_DIM] = jnp.zeros((KV_WIN, HEAD_DIM), BF16)
            vbuf[:, lo + HEAD_DIM:lo + 2 * HEAD_DIM] = jnp.ones((KV_WIN + tb, HEAD_DIM), BF16)

    @pl.when(i > 0)
    def _():
        kbuf[0:KV_WIN, :] = kbuf[tb:tb + KV_WIN, :]
        for h in range(N_HEADS):
            lo = 2 * h * HEAD_DIM
            vbuf[0:KV_WIN, lo:lo + HEAD_DIM] = vbuf[tb:tb + KV_WIN, lo:lo + HEAD_DIM]

    kbuf[KV_WIN:KV_WIN + tb, :] = k_ref[...]
    for h in range(N_HEADS):
        vbuf[KV_WIN:KV_WIN + tb, 2 * h * HEAD_DIM:(2 * h + 1) * HEAD_DIM] = v_ref[:, h * HEAD_DIM:(h + 1) * HEAD_DIM]

    _gmlp_block(u_ref, va_ref, wm_ref, sbb_ref, goa_ref, cat_ref, a_scr)

    @pl.when(i == 0)
    def _():
        _attn_groups(q_ref, kbuf, vbuf, bias_ref, gob_ref, cat_ref, o_scr, first=True)

    @pl.when(i > 0)
    def _():
        _attn_groups(q_ref, kbuf, vbuf, bias_ref, gob_ref, cat_ref, o_scr, first=False)


def _mixer_prompt(u, va, q, k, v, sw, sbb, bias, goa, gob, *, tb):
    m = u.shape[0]
    assert tb >= KV_WIN and tb % GROUP_Q == 0
    row_blk = lambda i: (i, 0)
    const2 = lambda i: (0, 0)
    const3 = lambda i: (0, 0, 0)
    return pl.pallas_call(
        _mixer_prompt_kernel,
        grid=(m // tb,),
        in_specs=[
            pl.BlockSpec((tb, D_A), row_blk),
            pl.BlockSpec((tb, D_A), row_blk),
            pl.BlockSpec((tb, D_B), row_blk),
            pl.BlockSpec((tb, D_B), row_blk),
            pl.BlockSpec((tb, D_B), row_blk),
            pl.BlockSpec((GROUPS_A, MLP_CHUNK, MLP_CHUNK), const3),
            pl.BlockSpec((GROUPS_A, MLP_CHUNK, GROUP_A), const3),
            pl.BlockSpec((N_HEADS, GROUP_Q, GROUP_K), const3),
            pl.BlockSpec((1, D_A), const2),
            pl.BlockSpec((1, D_B), const2),
        ],
        out_specs=pl.BlockSpec((tb, D_MODEL), row_blk),
        out_shape=jax.ShapeDtypeStruct((m, D_MODEL), BF16),
        scratch_shapes=[
            pltpu.VMEM((GROUPS_A, MLP_CHUNK, MLP_CHUNK), BF16),
            pltpu.VMEM((KV_WIN + tb, D_B), BF16),
            pltpu.VMEM((KV_WIN + tb, 2 * D_B), BF16),
            pltpu.VMEM((tb, D_A), F32),
            pltpu.VMEM((GROUP_Q, D_B), F32),
        ],
        compiler_params=_params(("arbitrary",)),
        name="mixer_prompt",
    )(u, va, q, k, v, sw, sbb, bias, goa, gob)


def _mixer_sample_kernel(u_ref, va_ref, q_ref, k_ref, v_ref, ck_ref, cv_ref, sw_ref, sbb_ref, bias_ref,
                         goa_ref, gob_ref, cat_ref, wm_ref, vap, kbuf, vbuf, a_scr, o_scr):
    b = pl.program_id(0)
    n = u_ref.shape[0]
    w = ck_ref.shape[1] // N_HEADS

    @pl.when(b == 0)
    def _():
        _mask_spatial_weights(sw_ref, wm_ref)
        vap[...] = jnp.zeros(vap.shape, BF16)

    vap[0:n, :] = va_ref[...].astype(BF16)
    for h in range(N_HEADS):
        cols = slice(h * HEAD_DIM, (h + 1) * HEAD_DIM)
        kbuf[0:w, cols] = ck_ref[0, pl.ds(h, w, stride=N_HEADS), :].astype(BF16)
        vbuf[0:w, cols] = cv_ref[0, pl.ds(h, w, stride=N_HEADS), :].astype(BF16)
    kbuf[w:w + n, :] = k_ref[...].astype(BF16)
    vbuf[w:w + n, :] = v_ref[...].astype(BF16)

    ssq = jnp.zeros((n, 1), F32)
    for g in range(GROUPS_A):
        cols = slice(g * GROUP_A, (g + 1) * GROUP_A)
        mixed = jnp.dot(wm_ref[g, 0:n, :], vap[:, cols], preferred_element_type=F32) + sbb_ref[g, 0:n, :]
        a = u_ref[:, cols] * mixed
        a_scr[:, cols] = a
        ssq = ssq + jnp.sum(a * a, axis=-1, keepdims=True)
    inv = lax.rsqrt(ssq * (1.0 / D_A) + EPS)
    cat_ref[:, 0:D_A] = (a_scr[...] * inv * goa_ref[...]).astype(BF16)

    ssq = jnp.zeros((n, 1), F32)
    for h in range(N_HEADS):
        cols = slice(h * HEAD_DIM, (h + 1) * HEAD_DIM)
        o = _attend(q_ref[:, cols], kbuf[:, cols], vbuf[:, cols], bias_ref[h], None)
        o_scr[:, cols] = o
        ssq = ssq + jnp.sum(o * o, axis=-1, keepdims=True)
    inv = lax.rsqrt(ssq * (1.0 / D_B) + EPS)
    cat_ref[:, D_A:D_A + D_B] = (o_scr[...] * inv * gob_ref[...]).astype(BF16)


def _mixer_sample(u, va, q, k, v, ck, cv, sw, sbb, bias, goa, gob, *, n):
    m = u.shape[0]
    nb, wh, _ = ck.shape
    w = wh // N_HEADS
    row_blk = lambda b: (b, 0)
    const2 = lambda b: (0, 0)
    const3 = lambda b: (0, 0, 0)
    return pl.pallas_call(
        _mixer_sample_kernel,
        grid=(nb,),
        in_specs=[
            pl.BlockSpec((n, D_A), row_blk),
            pl.BlockSpec((n, D_A), row_blk),
            pl.BlockSpec((n, D_B), row_blk),
            pl.BlockSpec((n, D_B), row_blk),
            pl.BlockSpec((n, D_B), row_blk),
            pl.BlockSpec((1, wh, HEAD_DIM), lambda b: (b, 0, 0)),
            pl.BlockSpec((1, wh, HEAD_DIM), lambda b: (b, 0, 0)),
            pl.BlockSpec((GROUPS_A, MLP_CHUNK, MLP_CHUNK), const3),
            pl.BlockSpec((GROUPS_A, MLP_CHUNK, GROUP_A), const3),
            pl.BlockSpec((N_HEADS, n, w + n), const3),
            pl.BlockSpec((1, D_A), const2),
            pl.BlockSpec((1, D_B), const2),
        ],
        out_specs=pl.BlockSpec((n, D_MODEL), row_blk),
        out_shape=jax.ShapeDtypeStruct((m, D_MODEL), BF16),
        scratch_shapes=[
            pltpu.VMEM((GROUPS_A, MLP_CHUNK, MLP_CHUNK), BF16),
            pltpu.VMEM((MLP_CHUNK, D_A), BF16),
            pltpu.VMEM((w + n, D_B), BF16),
            pltpu.VMEM((w + n, D_B), BF16),
            pltpu.VMEM((n, D_A), F32),
            pltpu.VMEM((n, D_B), F32),
        ],
        compiler_params=_params(("arbitrary",)),
        name="mixer_sample",
    )(u, va, q, k, v, ck, cv, sw, sbb, bias, goa, gob)


def _outproj_kernel(cat_ref, h_ref, w_ref, g_ref, o_ref, *, rc):
    for r in range(cat_ref.shape[0] // rc):
        rows = slice(r * rc, (r + 1) * rc)
        m = jnp.dot(cat_ref[rows, :], w_ref[...], preferred_element_type=F32)
        o_ref[rows, :] = h_ref[rows, :] + _rms(m, g_ref[...])


def _outproj(cat, h, w_out, g, *, tm):
    m = cat.shape[0]
    return pl.pallas_call(
        functools.partial(_outproj_kernel, rc=min(tm, PROJ_ROW_CHUNK)),
        grid=(m // tm,),
        in_specs=[
            pl.BlockSpec((tm, D_MODEL), lambda i: (i, 0)),
            pl.BlockSpec((tm, D_MODEL), lambda i: (i, 0)),
            pl.BlockSpec((D_MODEL, D_MODEL), lambda i: (0, 0), pipeline_mode=pl.Buffered(1)),
            pl.BlockSpec((1, D_MODEL), lambda i: (0, 0)),
        ],
        out_specs=pl.BlockSpec((tm, D_MODEL), lambda i: (i, 0)),
        out_shape=jax.ShapeDtypeStruct((m, D_MODEL), F32),
        compiler_params=_params(("parallel",)),
        name="outproj",
    )(cat, h, w_out, g)


def _rel_bias_table(rel_bias, lq, lk, band_chunks=None):
    nh = rel_bias.shape[0]
    p = lq + lk - 1
    t_min, t_max = KV_WIN - lk + 1, KV_WIN + lq - 1
    n_lo = max(0, -REL_CLIP - t_min)
    n_hi = max(0, t_max - REL_CLIP)
    mid = rel_bias[:, max(t_min, -REL_CLIP) + REL_CLIP:min(t_max, REL_CLIP) + REL_CLIP + 1]
    f = jnp.concatenate([jnp.broadcast_to(rel_bias[:, :1], (nh, n_lo)), mid,
                         jnp.broadcast_to(rel_bias[:, -1:], (nh, n_hi))], axis=1).astype(F32)
    assert f.shape == (nh, p)
    w = jnp.concatenate([f[:, ::-1], jnp.zeros((nh, 1), F32)], axis=1)
    skew = jnp.broadcast_to(w[:, None, :], (nh, lq, p + 1)).reshape(nh, lq * (p + 1))[:, :lq * p]
    table = skew.reshape(nh, lq, p)[:, :, lq - 1:lq - 1 + lk]
    if band_chunks is not None:
        qc = jnp.arange(lq)[:, None] // CHUNK
        jc = jnp.arange(lk)[None, :] // CHUNK
        table = jnp.where((jc >= qc) & (jc <= qc + band_chunks), table, NEG_INF)
    return table


def kernel(x_prompt, x_sample, cache_b_k, cache_b_v, w_in, w_out, sw_a, sb_a, rel_bias, g_va, g_oa, g_ob,
           g_f1_pre, g_f1_post, g_m_pre, g_m_post, g_f2_pre, g_f2_post,
           f1_gate, f1_up, f1_down, f2_gate, f2_up, f2_down):
    depth = w_in.shape[0]
    assert depth == 1
    bp, sp, _ = x_prompt.shape
    bs, ns, _ = x_sample.shape
    assert bp == 1
    l = 0
    row = lambda g: g[l].reshape(1, -1)
    w_in_b = w_in[l].astype(BF16)
    w_out_b = w_out[l].astype(BF16)
    sbb = jnp.broadcast_to(sb_a[l][:, :, None], (GROUPS_A, MLP_CHUNK, GROUP_A))
    bias_p = _rel_bias_table(rel_bias[l], GROUP_Q, GROUP_K, band_chunks=LEFT_CHUNKS) * LOG2E
    wc = cache_b_k.shape[2]
    bias_s = _rel_bias_table(rel_bias[l], ns, wc + ns)

    xp = x_prompt.reshape(sp, D_MODEL)
    h_head, *f1 = _ffn_head(xp, row(g_f1_pre), row(g_f1_post), f1_gate[l], f1_up[l], f1_down[l],
                            tm=1024, tf=512)
    h = _ffn(xp, row(g_f1_pre), row(g_f1_post), *f1, tm=1024, tf=512, resume=h_head)
    u, va, q, k, v, kt, vt = _inproj(h, row(g_m_pre), w_in_b, row(g_va), tm=512,
                                     va_dtype=BF16, kv_dtype=BF16, tail=True)
    cat = _mixer_prompt(u, va, q, k, v, sw_a[l], sbb, bias_p, row(g_oa), row(g_ob), tb=512)
    h2 = _outproj(cat, h, w_out_b, row(g_m_post), tm=512)
    y_head, *f2 = _ffn_head(h2, row(g_f2_pre), row(g_f2_post), f2_gate[l], f2_up[l], f2_down[l],
                            tm=1024, tf=512)
    yp = _ffn(h2, row(g_f2_pre), row(g_f2_post), *f2, tm=1024, tf=512, resume=y_head)

    ms = bs * ns
    xs = x_sample.reshape(ms, D_MODEL)
    hs = _ffn(xs, row(g_f1_pre), row(g_f1_post), *f1, tm=ms, tf=512)
    us, vas, qs, ks, vs = _inproj(hs, row(g_m_pre), w_in_b, row(g_va), tm=ms,
                                  va_dtype=F32, kv_dtype=F32, tail=False)
    ck = cache_b_k[l].reshape(bs, wc * N_HEADS, HEAD_DIM)
    cv = cache_b_v[l].reshape(bs, wc * N_HEADS, HEAD_DIM)
    cats = _mixer_sample(us, vas, qs, ks, vs, ck, cv, sw_a[l], sbb, bias_s, row(g_oa), row(g_ob), n=ns)
    hs2 = _outproj(cats, hs, w_out_b, row(g_m_post), tm=ms)
    ys = _ffn(hs2, row(g_f2_pre), row(g_f2_post), *f2, tm=ms, tf=512)

    hd = (N_HEADS, HEAD_DIM)
    return (
        yp.reshape(bp, sp, D_MODEL),
        ys.reshape(bs, ns, D_MODEL),
        kt.reshape(depth, bp, KV_WIN, *hd),
        vt.reshape(depth, bp, KV_WIN, *hd),
        ks.reshape(depth, bs, ns, *hd),
        vs.reshape(depth, bs, ns, *hd),
        vas.reshape(depth, bs, ns, D_A),
    )
```

```python
import functools

import jax
import jax.numpy as jnp
from jax import lax
from jax.experimental import pallas as pl
from jax.experimental.pallas import tpu as pltpu

D_MODEL = 2048
D_A = 1024
D_B = 1024
GROUPS_A = 8
GROUP_A = 128
MLP_CHUNK = 128
N_HEADS = 8
HEAD_DIM = 128
CHUNK = 64
LEFT_CHUNKS = 8
KV_WIN = LEFT_CHUNKS * CHUNK
BAND = (LEFT_CHUNKS + 1) * CHUNK
GROUP_CHUNKS = 4
GROUP_Q = GROUP_CHUNKS * CHUNK
GROUP_K = GROUP_Q + KV_WIN
REL_CLIP = 128
D_FF = 5632
D_IN = 2 * D_A + 3 * D_B
EPS = 1e-6
NEG_INF = -1e30
SCALE = HEAD_DIM ** -0.5
LOG2E = 1.4426950408889634

F32 = jnp.float32
BF16 = jnp.bfloat16

VMEM_LIMIT_BYTES = 60 * 1024 * 1024
FFN_ROW_CHUNK = 256
PROJ_ROW_CHUNK = 256


def _rms(x, g):
    return x * lax.rsqrt(jnp.mean(x * x, axis=-1, keepdims=True) + EPS) * g


def _params(sem):
    return pltpu.CompilerParams(dimension_semantics=sem, vmem_limit_bytes=VMEM_LIMIT_BYTES)


def _ffn_kernel(*refs, rc, convert, has_head):
    if convert:
        x_ref, gpre_ref, gpost_ref, wgf_ref, wuf_ref, wdf_ref, o_ref, wg_ref, wu_ref, wd_ref, xn_ref = refs
    elif has_head:
        x_ref, gpre_ref, gpost_ref, wg_ref, wu_ref, wd_ref, head_ref, o_ref, xn_ref = refs
    else:
        x_ref, gpre_ref, gpost_ref, wg_ref, wu_ref, wd_ref, o_ref, xn_ref = refs
    i = pl.program_id(0)
    j = pl.program_id(1)
    nj = pl.num_programs(1)
    tm = x_ref.shape[0]

    def body(first, last):
        if convert:
            wg_ref[0] = wgf_ref[...].astype(BF16)
            wu_ref[0] = wuf_ref[...].astype(BF16)
            wd_ref[...] = wdf_ref[...].astype(BF16)
        for r in range(tm // rc):
            rows = slice(r * rc, (r + 1) * rc)
            if first:
                xn_ref[rows, :] = _rms(x_ref[rows, :], gpre_ref[...]).astype(BF16)
            xn = xn_ref[rows, :]
            g = jnp.dot(xn, wg_ref[0], preferred_element_type=F32)
            u = jnp.dot(xn, wu_ref[0], preferred_element_type=F32)
            a = (g * jax.nn.sigmoid(g) * u).astype(BF16)
            d = jnp.dot(a, wd_ref[...], preferred_element_type=F32)
            if first:
                o_ref[rows, :] = d
            elif last:
                o_ref[rows, :] = x_ref[rows, :] + 0.5 * _rms(o_ref[rows, :] + d, gpost_ref[...])
            else:
                o_ref[rows, :] += d

    compute = i > 0 if has_head else True
    pl.when(jnp.logical_and(compute, j == 0))(functools.partial(body, True, False))
    pl.when(jnp.logical_and(compute, jnp.logical_and(j > 0, j < nj - 1)))(functools.partial(body, False, False))
    pl.when(jnp.logical_and(compute, j == nj - 1))(functools.partial(body, False, True))

    if has_head:
        @pl.when(jnp.logical_and(i == 0, j == 0))
        def _():
            pltpu.sync_copy(head_ref, o_ref)


def _ffn_head(x, gpre, gpost, wg, wu, wd, *, tm, tf):
    nj = D_FF // tf
    assert nj >= 2
    vec = pl.BlockSpec((1, D_MODEL), lambda i, j: (0, 0))
    return pl.pallas_call(
        functools.partial(_ffn_kernel, rc=min(tm, FFN_ROW_CHUNK), convert=True, has_head=False),
        grid=(1, nj),
        in_specs=[
            pl.BlockSpec((tm, D_MODEL), lambda i, j: (0, 0), pipeline_mode=pl.Buffered(1)), vec, vec,
            pl.BlockSpec((D_MODEL, tf), lambda i, j: (0, j)),
            pl.BlockSpec((D_MODEL, tf), lambda i, j: (0, j)),
            pl.BlockSpec((tf, D_MODEL), lambda i, j: (j, 0)),
        ],
        out_specs=[
            pl.BlockSpec((tm, D_MODEL), lambda i, j: (0, 0)),
            pl.BlockSpec((1, D_MODEL, tf), lambda i, j: (j, 0, 0)),
            pl.BlockSpec((1, D_MODEL, tf), lambda i, j: (j, 0, 0)),
            pl.BlockSpec((tf, D_MODEL), lambda i, j: (j, 0)),
        ],
        out_shape=[
            jax.ShapeDtypeStruct((tm, D_MODEL), F32),
            jax.ShapeDtypeStruct((nj, D_MODEL, tf), BF16),
            jax.ShapeDtypeStruct((nj, D_MODEL, tf), BF16),
            jax.ShapeDtypeStruct((D_FF, D_MODEL), BF16),
        ],
        scratch_shapes=[pltpu.VMEM((tm, D_MODEL), BF16)],
        compiler_params=_params(("arbitrary", "arbitrary")),
        name="ffn_head",
    )(x, gpre, gpost, wg, wu, wd)


def _ffn(x, gpre, gpost, wg, wu, wd, *, tm, head=None):
    m = x.shape[0]
    nj, _, tf = wg.shape
    has_head = head is not None
    if has_head:
        assert head.shape == (tm, D_MODEL) and m // tm >= 2
        x_idx = lambda i, j: (jnp.maximum(i, 1), 0)
        hid = lambda i, j: jnp.where(i == 0, 0, j)
    else:
        x_idx = lambda i, j: (i, 0)
        hid = lambda i, j: j
    vec = pl.BlockSpec((1, D_MODEL), lambda i, j: (0, 0))
    in_specs = [
        pl.BlockSpec((tm, D_MODEL), x_idx), vec, vec,
        pl.BlockSpec((1, D_MODEL, tf), lambda i, j: (hid(i, j), 0, 0)),
        pl.BlockSpec((1, D_MODEL, tf), lambda i, j: (hid(i, j), 0, 0)),
        pl.BlockSpec((tf, D_MODEL), lambda i, j: (hid(i, j), 0)),
    ]
    args = [x, gpre, gpost, wg, wu, wd]
    if has_head:
        in_specs.append(pl.BlockSpec(memory_space=pl.ANY))
        args.append(head)
    return pl.pallas_call(
        functools.partial(_ffn_kernel, rc=min(tm, FFN_ROW_CHUNK), convert=False, has_head=has_head),
        grid=(m // tm, nj),
        in_specs=in_specs,
        out_specs=pl.BlockSpec((tm, D_MODEL), lambda i, j: (i, 0)),
        out_shape=jax.ShapeDtypeStruct((m, D_MODEL), F32),
        scratch_shapes=[pltpu.VMEM((tm, D_MODEL), BF16)],
        compiler_params=_params(("arbitrary", "arbitrary")),
        name="ffn",
    )(*args)


def _inproj_kernel(h_ref, gpre_ref, w_ref, gva_ref, u_ref, va_ref, q_ref, k_ref, v_ref, *rest, rc, tail):
    if tail:
        kt_ref, vt_ref, hn_ref = rest
    else:
        (hn_ref,) = rest
    tm = h_ref.shape[0]
    for r in range(tm // rc):
        rows = slice(r * rc, (r + 1) * rc)
        hn_ref[rows, :] = _rms(h_ref[rows, :], gpre_ref[...]).astype(BF16)
        hn = hn_ref[rows, :]
        z = lambda c: jnp.dot(hn, w_ref[:, c * D_A:(c + 1) * D_A], preferred_element_type=F32)
        u_ref[rows, :] = jax.nn.gelu(z(0))
        va_ref[rows, :] = _rms(jax.nn.gelu(z(1)), gva_ref[...]).astype(va_ref.dtype)
        q_ref[rows, :] = z(2).astype(q_ref.dtype)
        for c, o_ref, t_ref in ((3, k_ref, kt_ref if tail else None), (4, v_ref, vt_ref if tail else None)):
            zc = z(c)
            o_ref[rows, :] = zc.astype(o_ref.dtype)
            t0 = r * rc - (tm - KV_WIN)
            if tail and t0 >= 0:
                t_ref[t0:t0 + rc, :] = zc


def _inproj(h, gpre, w_in, gva, *, tm, va_dtype, kv_dtype, tail):
    m = h.shape[0]
    rc = min(tm, PROJ_ROW_CHUNK)
    row_blk = lambda i: (i, 0)
    const = lambda i: (0, 0)
    out_specs = [pl.BlockSpec((tm, D_A), row_blk)] * 5
    out_shape = [
        jax.ShapeDtypeStruct((m, D_A), F32),
        jax.ShapeDtypeStruct((m, D_A), va_dtype),
        jax.ShapeDtypeStruct((m, D_B), BF16),
        jax.ShapeDtypeStruct((m, D_B), kv_dtype),
        jax.ShapeDtypeStruct((m, D_B), kv_dtype),
    ]
    if tail:
        assert tm >= KV_WIN and (tm - KV_WIN) % rc == 0
        out_specs += [pl.BlockSpec((KV_WIN, D_B), const)] * 2
        out_shape += [jax.ShapeDtypeStruct((KV_WIN, D_B), F32)] * 2
    return pl.pallas_call(
        functools.partial(_inproj_kernel, rc=rc, tail=tail),
        grid=(m // tm,),
        in_specs=[
            pl.BlockSpec((tm, D_MODEL), row_blk),
            pl.BlockSpec((1, D_MODEL), const),
            pl.BlockSpec((D_MODEL, D_IN), const, pipeline_mode=pl.Buffered(1)),
            pl.BlockSpec((1, D_A), const),
        ],
        out_specs=out_specs,
        out_shape=out_shape,
        scratch_shapes=[pltpu.VMEM((tm, D_MODEL), BF16)],
        compiler_params=_params(("arbitrary",)),
        name="inproj",
    )(h, gpre, w_in, gva)


def _mask_spatial_weights(sw_ref, wm_ref):
    t = lax.broadcasted_iota(jnp.int32, (MLP_CHUNK, MLP_CHUNK), 0)
    s = lax.broadcasted_iota(jnp.int32, (MLP_CHUNK, MLP_CHUNK), 1)
    for g in range(GROUPS_A):
        wm_ref[g] = jnp.where(s <= t, sw_ref[g], 0.0).astype(BF16)


def _attend(qh, kh, vh, bias, valid):
    s = lax.dot_general(qh, kh, (((1,), (1,)), ((), ())), preferred_element_type=F32) * SCALE + bias
    if valid is not None:
        s = jnp.where(valid, s, NEG_INF)
    e = jnp.exp(s - jnp.max(s, axis=-1, keepdims=True))
    p = e / jnp.sum(e, axis=-1, keepdims=True)
    return jnp.dot(p.astype(BF16), vh, preferred_element_type=F32)


def _gmlp_block(u_ref, va_ref, wm_ref, sbb_ref, goa_ref, cat_ref, a_scr):
    nchunk = u_ref.shape[0] // MLP_CHUNK
    chunk_rows = [slice(n * MLP_CHUNK, (n + 1) * MLP_CHUNK) for n in range(nchunk)]
    for g in range(GROUPS_A):
        cols = slice(g * GROUP_A, (g + 1) * GROUP_A)
        v_all = jnp.concatenate([va_ref[rows, cols] for rows in chunk_rows], axis=1)
        mixed = jnp.dot(wm_ref[g], v_all, preferred_element_type=F32)
        for n, rows in enumerate(chunk_rows):
            a_scr[rows, cols] = u_ref[rows, cols] * (mixed[:, n * GROUP_A:(n + 1) * GROUP_A] + sbb_ref[g])
    for rows in chunk_rows:
        cat_ref[rows, 0:D_A] = _rms(a_scr[rows, :], goa_ref[...]).astype(BF16)


def _attn_groups(q_ref, kbuf, vbuf, bias_ref, gob_ref, cat_ref, o_scr, *, first):
    tb = q_ref.shape[0]
    for gq in range(tb // GROUP_Q):
        q0 = gq * GROUP_Q
        qrows = slice(q0, q0 + GROUP_Q)
        krows = slice(q0, q0 + GROUP_K)
        if first:
            valid = lax.broadcasted_iota(jnp.int32, (GROUP_Q, GROUP_K), 1) >= KV_WIN - q0
        for h in range(N_HEADS):
            cols = slice(h * HEAD_DIM, (h + 1) * HEAD_DIM)
            s2 = lax.dot_general(q_ref[qrows, cols], kbuf[krows, cols], (((1,), (1,)), ((), ())),
                                 preferred_element_type=F32) * (SCALE * LOG2E) + bias_ref[h]
            if first:
                s2 = jnp.where(valid, s2, NEG_INF)
            e = jnp.exp2(s2 - jnp.max(s2, axis=-1, keepdims=True)).astype(BF16)
            nd = jnp.dot(e, vbuf[krows, 2 * h * HEAD_DIM:2 * (h + 1) * HEAD_DIM], preferred_element_type=F32)
            o_scr[:, cols] = nd[:, 0:HEAD_DIM] * (1.0 / nd[:, HEAD_DIM:2 * HEAD_DIM])
        cat_ref[qrows, D_A:D_A + D_B] = _rms(o_scr[...], gob_ref[...]).astype(BF16)


def _mixer_prompt_kernel(u_ref, va_ref, q_ref, k_ref, v_ref, sw_ref, sbb_ref, bias_ref, goa_ref, gob_ref,
                         cat_ref, wm_ref, kbuf, vbuf, a_scr, o_scr):
    i = pl.program_id(0)
    tb = u_ref.shape[0]

    @pl.when(i == 0)
    def _():
        _mask_spatial_weights(sw_ref, wm_ref)
        kbuf[0:KV_WIN, :] = jnp.zeros((KV_WIN, D_B), BF16)
        for h in range(N_HEADS):
            lo = 2 * h * HEAD_DIM
            vbuf[0:KV_WIN, lo:lo + HEAD_DIM] = jnp.zeros((KV_WIN, HEAD_DIM), BF16)
            vbuf[:, lo + HEAD_DIM:lo + 2 * HEAD_DIM] = jnp.ones((KV_WIN + tb, HEAD_DIM), BF16)

    @pl.when(i > 0)
    def _():
        kbuf[0:KV_WIN, :] = kbuf[tb:tb + KV_WIN, :]
        for h in range(N_HEADS):
            lo = 2 * h * HEAD_DIM
            vbuf[0:KV_WIN, lo:lo + HEAD_DIM] = vbuf[tb:tb + KV_WIN, lo:lo + HEAD_DIM]

    kbuf[KV_WIN:KV_WIN + tb, :] = k_ref[...]
    for h in range(N_HEADS):
        vbuf[KV_WIN:KV_WIN + tb, 2 * h * HEAD_DIM:(2 * h + 1) * HEAD_DIM] = v_ref[:, h * HEAD_DIM:(h + 1) * HEAD_DIM]

    _gmlp_block(u_ref, va_ref, wm_ref, sbb_ref, goa_ref, cat_ref, a_scr)

    @pl.when(i == 0)
    def _():
        _attn_groups(q_ref, kbuf, vbuf, bias_ref, gob_ref, cat_ref, o_scr, first=True)

    @pl.when(i > 0)
    def _():
        _attn_groups(q_ref, kbuf, vbuf, bias_ref, gob_ref, cat_ref, o_scr, first=False)


def _mixer_prompt(u, va, q, k, v, sw, sbb, bias, goa, gob, *, tb):
    m = u.shape[0]
    assert tb >= KV_WIN and tb % GROUP_Q == 0
    row_blk = lambda i: (i, 0)
    const2 = lambda i: (0, 0)
    const3 = lambda i: (0, 0, 0)
    return pl.pallas_call(
        _mixer_prompt_kernel,
        grid=(m // tb,),
        in_specs=[
            pl.BlockSpec((tb, D_A), row_blk),
            pl.BlockSpec((tb, D_A), row_blk),
            pl.BlockSpec((tb, D_B), row_blk),
            pl.BlockSpec((tb, D_B), row_blk),
            pl.BlockSpec((tb, D_B), row_blk),
            pl.BlockSpec((GROUPS_A, MLP_CHUNK, MLP_CHUNK), const3),
            pl.BlockSpec((GROUPS_A, MLP_CHUNK, GROUP_A), const3),
            pl.BlockSpec((N_HEADS, GROUP_Q, GROUP_K), const3),
            pl.BlockSpec((1, D_A), const2),
            pl.BlockSpec((1, D_B), const2),
        ],
        out_specs=pl.BlockSpec((tb, D_MODEL), row_blk),
        out_shape=jax.ShapeDtypeStruct((m, D_MODEL), BF16),
        scratch_shapes=[
            pltpu.VMEM((GROUPS_A, MLP_CHUNK, MLP_CHUNK), BF16),
            pltpu.VMEM((KV_WIN + tb, D_B), BF16),
            pltpu.VMEM((KV_WIN + tb, 2 * D_B), BF16),
            pltpu.VMEM((tb, D_A), F32),
            pltpu.VMEM((GROUP_Q, D_B), F32),
        ],
        compiler_params=_params(("arbitrary",)),
        name="mixer_prompt",
    )(u, va, q, k, v, sw, sbb, bias, goa, gob)


def _mixer_sample_kernel(u_ref, va_ref, q_ref, k_ref, v_ref, ck_ref, cv_ref, sw_ref, sbb_ref, bias_ref,
                         goa_ref, gob_ref, cat_ref, wm_ref, vap, kbuf, vbuf, a_scr, o_scr):
    b = pl.program_id(0)
    n = u_ref.shape[0]
    w = ck_ref.shape[1] // N_HEADS

    @pl.when(b == 0)
    def _():
        _mask_spatial_weights(sw_ref, wm_ref)
        vap[...] = jnp.zeros(vap.shape, BF16)

    vap[0:n, :] = va_ref[...].astype(BF16)
    for h in range(N_HEADS):
        cols = slice(h * HEAD_DIM, (h + 1) * HEAD_DIM)
        kbuf[0:w, cols] = ck_ref[0, pl.ds(h, w, stride=N_HEADS), :].astype(BF16)
        vbuf[0:w, cols] = cv_ref[0, pl.ds(h, w, stride=N_HEADS), :].astype(BF16)
    kbuf[w:w + n, :] = k_ref[...].astype(BF16)
    vbuf[w:w + n, :] = v_ref[...].astype(BF16)

    ssq = jnp.zeros((n, 1), F32)
    for g in range(GROUPS_A):
        cols = slice(g * GROUP_A, (g + 1) * GROUP_A)
        mixed = jnp.dot(wm_ref[g, 0:n, :], vap[:, cols], preferred_element_type=F32) + sbb_ref[g, 0:n, :]
        a = u_ref[:, cols] * mixed
        a_scr[:, cols] = a
        ssq = ssq + jnp.sum(a * a, axis=-1, keepdims=True)
    inv = lax.rsqrt(ssq * (1.0 / D_A) + EPS)
    cat_ref[:, 0:D_A] = (a_scr[...] * inv * goa_ref[...]).astype(BF16)

    ssq = jnp.zeros((n, 1), F32)
    for h in range(N_HEADS):
        cols = slice(h * HEAD_DIM, (h + 1) * HEAD_DIM)
        o = _attend(q_ref[:, cols], kbuf[:, cols], vbuf[:, cols], bias_ref[h], None)
        o_scr[:, cols] = o
        ssq = ssq + jnp.sum(o * o, axis=-1, keepdims=True)
    inv = lax.rsqrt(ssq * (1.0 / D_B) + EPS)
    cat_ref[:, D_A:D_A + D_B] = (o_scr[...] * inv * gob_ref[...]).astype(BF16)


def _mixer_sample(u, va, q, k, v, ck, cv, sw, sbb, bias, goa, gob, *, n):
    m = u.shape[0]
    nb, wh, _ = ck.shape
    w = wh // N_HEADS
    row_blk = lambda b: (b, 0)
    const2 = lambda b: (0, 0)
    const3 = lambda b: (0, 0, 0)
    return pl.pallas_call(
        _mixer_sample_kernel,
        grid=(nb,),
        in_specs=[
            pl.BlockSpec((n, D_A), row_blk),
            pl.BlockSpec((n, D_A), row_blk),
            pl.BlockSpec((n, D_B), row_blk),
            pl.BlockSpec((n, D_B), row_blk),
            pl.BlockSpec((n, D_B), row_blk),
            pl.BlockSpec((1, wh, HEAD_DIM), lambda b: (b, 0, 0)),
            pl.BlockSpec((1, wh, HEAD_DIM), lambda b: (b, 0, 0)),
            pl.BlockSpec((GROUPS_A, MLP_CHUNK, MLP_CHUNK), const3),
            pl.BlockSpec((GROUPS_A, MLP_CHUNK, GROUP_A), const3),
            pl.BlockSpec((N_HEADS, n, w + n), const3),
            pl.BlockSpec((1, D_A), const2),
            pl.BlockSpec((1, D_B), const2),
        ],
        out_specs=pl.BlockSpec((n, D_MODEL), row_blk),
        out_shape=jax.ShapeDtypeStruct((m, D_MODEL), BF16),
        scratch_shapes=[
            pltpu.VMEM((GROUPS_A, MLP_CHUNK, MLP_CHUNK), BF16),
            pltpu.VMEM((MLP_CHUNK, D_A), BF16),
            pltpu.VMEM((w + n, D_B), BF16),
            pltpu.VMEM((w + n, D_B), BF16),
            pltpu.VMEM((n, D_A), F32),
            pltpu.VMEM((n, D_B), F32),
        ],
        compiler_params=_params(("arbitrary",)),
        name="mixer_sample",
    )(u, va, q, k, v, ck, cv, sw, sbb, bias, goa, gob)


def _outproj_kernel(cat_ref, h_ref, w_ref, g_ref, o_ref, *, rc):
    for r in range(cat_ref.shape[0] // rc):
        rows = slice(r * rc, (r + 1) * rc)
        m = jnp.dot(cat_ref[rows, :], w_ref[...], preferred_element_type=F32)
        o_ref[rows, :] = h_ref[rows, :] + _rms(m, g_ref[...])


def _outproj(cat, h, w_out, g, *, tm):
    m = cat.shape[0]
    return pl.pallas_call(
        functools.partial(_outproj_kernel, rc=min(tm, PROJ_ROW_CHUNK)),
        grid=(m // tm,),
        in_specs=[
            pl.BlockSpec((tm, D_MODEL), lambda i: (i, 0)),
            pl.BlockSpec((tm, D_MODEL), lambda i: (i, 0)),
            pl.BlockSpec((D_MODEL, D_MODEL), lambda i: (0, 0), pipeline_mode=pl.Buffered(1)),
            pl.BlockSpec((1, D_MODEL), lambda i: (0, 0)),
        ],
        out_specs=pl.BlockSpec((tm, D_MODEL), lambda i: (i, 0)),
        out_shape=jax.ShapeDtypeStruct((m, D_MODEL), F32),
        compiler_params=_params(("parallel",)),
        name="outproj",
    )(cat, h, w_out, g)


def _rel_bias_table(rel_bias, lq, lk, band_chunks=None):
    nh = rel_bias.shape[0]
    p = lq + lk - 1
    t_min, t_max = KV_WIN - lk + 1, KV_WIN + lq - 1
    n_lo = max(0, -REL_CLIP - t_min)
    n_hi = max(0, t_max - REL_CLIP)
    mid = rel_bias[:, max(t_min, -REL_CLIP) + REL_CLIP:min(t_max, REL_CLIP) + REL_CLIP + 1]
    f = jnp.concatenate([jnp.broadcast_to(rel_bias[:, :1], (nh, n_lo)), mid,
                         jnp.broadcast_to(rel_bias[:, -1:], (nh, n_hi))], axis=1).astype(F32)
    assert f.shape == (nh, p)
    w = jnp.concatenate([f[:, ::-1], jnp.zeros((nh, 1), F32)], axis=1)
    skew = jnp.broadcast_to(w[:, None, :], (nh, lq, p + 1)).reshape(nh, lq * (p + 1))[:, :lq * p]
    table = skew.reshape(nh, lq, p)[:, :, lq - 1:lq - 1 + lk]
    if band_chunks is not None:
        qc = jnp.arange(lq)[:, None] // CHUNK
        jc = jnp.arange(lk)[None, :] // CHUNK
        table = jnp.where((jc >= qc) & (jc <= qc + band_chunks), table, NEG_INF)
    return table


def kernel(x_prompt, x_sample, cache_b_k, cache_b_v, w_in, w_out, sw_a, sb_a, rel_bias, g_va, g_oa, g_ob,
           g_f1_pre, g_f1_post, g_m_pre, g_m_post, g_f2_pre, g_f2_post,
           f1_gate, f1_up, f1_down, f2_gate, f2_up, f2_down):
    depth = w_in.shape[0]
    assert depth == 1
    bp, sp, _ = x_prompt.shape
    bs, ns, _ = x_sample.shape
    assert bp == 1
    l = 0
    row = lambda g: g[l].reshape(1, -1)
    w_in_b = w_in[l].astype(BF16)
    w_out_b = w_out[l].astype(BF16)
    sbb = jnp.broadcast_to(sb_a[l][:, :, None], (GROUPS_A, MLP_CHUNK, GROUP_A))
    bias_p = _rel_bias_table(rel_bias[l], GROUP_Q, GROUP_K, band_chunks=LEFT_CHUNKS) * LOG2E
    wc = cache_b_k.shape[2]
    bias_s = _rel_bias_table(rel_bias[l], ns, wc + ns)

    xp = x_prompt.reshape(sp, D_MODEL)
    h_head, *f1 = _ffn_head(xp, row(g_f1_pre), row(g_f1_post), f1_gate[l], f1_up[l], f1_down[l],
                            tm=1024, tf=512)
    h = _ffn(xp, row(g_f1_pre), row(g_f1_post), *f1, tm=1024, head=h_head)
    u, va, q, k, v, kt, vt = _inproj(h, row(g_m_pre), w_in_b, row(g_va), tm=512,
                                     va_dtype=BF16, kv_dtype=BF16, tail=True)
    cat = _mixer_prompt(u, va, q, k, v, sw_a[l], sbb, bias_p, row(g_oa), row(g_ob), tb=512)
    h2 = _outproj(cat, h, w_out_b, row(g_m_post), tm=512)
    y_head, *f2 = _ffn_head(h2, row(g_f2_pre), row(g_f2_post), f2_gate[l], f2_up[l], f2_down[l],
                            tm=1024, tf=512)
    yp = _ffn(h2, row(g_f2_pre), row(g_f2_post), *f2, tm=1024, head=y_head)

    ms = bs * ns
    xs = x_sample.reshape(ms, D_MODEL)
    hs = _ffn(xs, row(g_f1_pre), row(g_f1_post), *f1, tm=ms)
    us, vas, qs, ks, vs = _inproj(hs, row(g_m_pre), w_in_b, row(g_va), tm=ms,
                                  va_dtype=F32, kv_dtype=F32, tail=False)
    ck = cache_b_k[l].reshape(bs, wc * N_HEADS, HEAD_DIM)
    cv = cache_b_v[l].reshape(bs, wc * N_HEADS, HEAD_DIM)
    cats = _mixer_sample(us, vas, qs, ks, vs, ck, cv, sw_a[l], sbb, bias_s, row(g_oa), row(g_ob), n=ns)
    hs2 = _outproj(cats, hs, w_out_b, row(g_m_post), tm=ms)
    ys = _ffn(hs2, row(g_f2_pre), row(g_f2_post), *f2, tm=ms)

    hd = (N_HEADS, HEAD_DIM)
    return (
        yp.reshape(bp, sp, D_MODEL),
        ys.reshape(bs, ns, D_MODEL),
        kt.reshape(depth, bp, KV_WIN, *hd),
        vt.reshape(depth, bp, KV_WIN, *hd),
        ks.reshape(depth, bs, ns, *hd),
        vs.reshape(depth, bs, ns, *hd),
        vas.reshape(depth, bs, ns, D_A),
    )
```

```python
import functools

import jax
import jax.numpy as jnp
from jax import lax
from jax.experimental import pallas as pl
from jax.experimental.pallas import tpu as pltpu

D_MODEL = 2048
D_A = 1024
D_B = 1024
GROUPS_A = 8
GROUP_A = 128
MLP_CHUNK = 128
N_HEADS = 8
HEAD_DIM = 128
CHUNK = 64
LEFT_CHUNKS = 8
KV_WIN = LEFT_CHUNKS * CHUNK
BAND = (LEFT_CHUNKS + 1) * CHUNK
GROUP_CHUNKS = 4
GROUP_Q = GROUP_CHUNKS * CHUNK
GROUP_K = GROUP_Q + KV_WIN
REL_CLIP = 128
D_FF = 5632
D_IN = 2 * D_A + 3 * D_B
EPS = 1e-6
NEG_INF = -1e30
SCALE = HEAD_DIM ** -0.5
LOG2E = 1.4426950408889634

F32 = jnp.float32
BF16 = jnp.bfloat16

VMEM_LIMIT_BYTES = 60 * 1024 * 1024
FFN_ROW_CHUNK = 256
PROJ_ROW_CHUNK = 256
FFN_ROW_TILE = 1024
FFN_HIDDEN_TILE = 512
FFN_HEAD_HIDDEN_TILE = 256


def _rms(x, g):
    return x * lax.rsqrt(jnp.mean(x * x, axis=-1, keepdims=True) + EPS) * g


def _params(sem):
    return pltpu.CompilerParams(dimension_semantics=sem, vmem_limit_bytes=VMEM_LIMIT_BYTES)


def _ffn_rows(x_ref, o_ref, xn_ref, gpre_ref, gpost_ref, wg_ref, wu_ref, wd_ref, *, rc, first, last):
    for r in range(x_ref.shape[0] // rc):
        rows = slice(r * rc, (r + 1) * rc)
        if first:
            xn_ref[rows, :] = _rms(x_ref[rows, :], gpre_ref[...]).astype(BF16)
        xn = xn_ref[rows, :]
        parts = []
        for t in range(wg_ref.shape[0]):
            g = jnp.dot(xn, wg_ref[t], preferred_element_type=F32)
            u = jnp.dot(xn, wu_ref[t], preferred_element_type=F32)
            parts.append((g * jax.nn.sigmoid(g) * u).astype(BF16))
        a = parts[0] if len(parts) == 1 else jnp.concatenate(parts, axis=1)
        d = jnp.dot(a, wd_ref[...], preferred_element_type=F32)
        if first:
            o_ref[rows, :] = d
        elif last:
            o_ref[rows, :] = x_ref[rows, :] + 0.5 * _rms(o_ref[rows, :] + d, gpost_ref[...])
        else:
            o_ref[rows, :] += d


def _when_first_mid_last(j, nj, enabled, body):
    pl.when(jnp.logical_and(enabled, j == 0))(functools.partial(body, first=True, last=False))
    pl.when(jnp.logical_and(enabled, jnp.logical_and(j > 0, j < nj - 1)))(
        functools.partial(body, first=False, last=False))
    pl.when(jnp.logical_and(enabled, j == nj - 1))(functools.partial(body, first=False, last=True))


def _ffn_head_kernel(xp_ref, xs_ref, gpre_ref, gpost_ref, wgf_ref, wuf_ref, wdf_ref,
                     op_ref, os_ref, wg_ref, wu_ref, wd_ref, xnp_ref, xns_ref, *, rc):
    def body(first, last):
        wg_ref[0] = wgf_ref[...].astype(BF16)
        wu_ref[0] = wuf_ref[...].astype(BF16)
        wd_ref[...] = wdf_ref[...].astype(BF16)
        for x_ref, o_ref, xn_ref in ((xp_ref, op_ref, xnp_ref), (xs_ref, os_ref, xns_ref)):
            _ffn_rows(x_ref, o_ref, xn_ref, gpre_ref, gpost_ref, wg_ref, wu_ref, wd_ref,
                      rc=min(rc, x_ref.shape[0]), first=first, last=last)

    _when_first_mid_last(pl.program_id(0), pl.num_programs(0), True, body)


def _ffn_head(xp, xs, gpre, gpost, wg, wu, wd, *, tm, tf):
    nj = D_FF // tf
    ms = xs.shape[0]
    assert nj >= 2
    const = lambda j: (0, 0)
    vec = pl.BlockSpec((1, D_MODEL), const)
    return pl.pallas_call(
        functools.partial(_ffn_head_kernel, rc=FFN_ROW_CHUNK),
        grid=(nj,),
        in_specs=[
            pl.BlockSpec((tm, D_MODEL), const, pipeline_mode=pl.Buffered(1)),
            pl.BlockSpec((ms, D_MODEL), const, pipeline_mode=pl.Buffered(1)),
            vec, vec,
            pl.BlockSpec((D_MODEL, tf), lambda j: (0, j)),
            pl.BlockSpec((D_MODEL, tf), lambda j: (0, j)),
            pl.BlockSpec((tf, D_MODEL), lambda j: (j, 0)),
        ],
        out_specs=[
            pl.BlockSpec((tm, D_MODEL), const),
            pl.BlockSpec((ms, D_MODEL), const),
            pl.BlockSpec((1, D_MODEL, tf), lambda j: (j, 0, 0)),
            pl.BlockSpec((1, D_MODEL, tf), lambda j: (j, 0, 0)),
            pl.BlockSpec((tf, D_MODEL), lambda j: (j, 0)),
        ],
        out_shape=[
            jax.ShapeDtypeStruct((tm, D_MODEL), F32),
            jax.ShapeDtypeStruct((ms, D_MODEL), F32),
            jax.ShapeDtypeStruct((nj, D_MODEL, tf), BF16),
            jax.ShapeDtypeStruct((nj, D_MODEL, tf), BF16),
            jax.ShapeDtypeStruct((D_FF, D_MODEL), BF16),
        ],
        scratch_shapes=[pltpu.VMEM((tm, D_MODEL), BF16), pltpu.VMEM((ms, D_MODEL), BF16)],
        compiler_params=_params(("arbitrary",)),
        name="ffn_head",
    )(xp, xs, gpre, gpost, wg, wu, wd)


def _ffn_kernel(x_ref, gpre_ref, gpost_ref, wg_ref, wu_ref, wd_ref, head_ref, o_ref, xn_ref, *, rc):
    i = pl.program_id(0)
    j = pl.program_id(1)
    body = functools.partial(_ffn_rows, x_ref, o_ref, xn_ref, gpre_ref, gpost_ref, wg_ref, wu_ref, wd_ref,
                             rc=rc)
    _when_first_mid_last(j, pl.num_programs(1), i > 0, body)

    @pl.when(jnp.logical_and(i == 0, j == 0))
    def _():
        pltpu.sync_copy(head_ref, o_ref)


def _ffn(x, gpre, gpost, wg, wu, wd, head, *, tm, tf):
    m = x.shape[0]
    sub = tf // wg.shape[2]
    nj = wg.shape[0] // sub
    assert head.shape == (tm, D_MODEL) and m // tm >= 2 and nj >= 2
    hid = lambda i, j: jnp.where(i == 0, 0, j)
    vec = pl.BlockSpec((1, D_MODEL), lambda i, j: (0, 0))
    return pl.pallas_call(
        functools.partial(_ffn_kernel, rc=FFN_ROW_CHUNK),
        grid=(m // tm, nj),
        in_specs=[
            pl.BlockSpec((tm, D_MODEL), lambda i, j: (jnp.maximum(i, 1), 0)), vec, vec,
            pl.BlockSpec((sub, D_MODEL, tf // sub), lambda i, j: (hid(i, j), 0, 0)),
            pl.BlockSpec((sub, D_MODEL, tf // sub), lambda i, j: (hid(i, j), 0, 0)),
            pl.BlockSpec((tf, D_MODEL), lambda i, j: (hid(i, j), 0)),
            pl.BlockSpec(memory_space=pl.ANY),
        ],
        out_specs=pl.BlockSpec((tm, D_MODEL), lambda i, j: (i, 0)),
        out_shape=jax.ShapeDtypeStruct((m, D_MODEL), F32),
        scratch_shapes=[pltpu.VMEM((tm, D_MODEL), BF16)],
        compiler_params=_params(("arbitrary", "arbitrary")),
        name="ffn",
    )(x, gpre, gpost, wg, wu, wd, head)


def _inproj_kernel(h_ref, gpre_ref, w_ref, gva_ref, u_ref, va_ref, q_ref, k_ref, v_ref, *rest, rc, tail):
    if tail:
        kt_ref, vt_ref, hn_ref = rest
    else:
        (hn_ref,) = rest
    tm = h_ref.shape[0]
    for r in range(tm // rc):
        rows = slice(r * rc, (r + 1) * rc)
        hn_ref[rows, :] = _rms(h_ref[rows, :], gpre_ref[...]).astype(BF16)
        hn = hn_ref[rows, :]
        z = lambda c: jnp.dot(hn, w_ref[:, c * D_A:(c + 1) * D_A], preferred_element_type=F32)
        u_ref[rows, :] = jax.nn.gelu(z(0))
        va_ref[rows, :] = _rms(jax.nn.gelu(z(1)), gva_ref[...]).astype(va_ref.dtype)
        q_ref[rows, :] = z(2).astype(q_ref.dtype)
        for c, o_ref, t_ref in ((3, k_ref, kt_ref if tail else None), (4, v_ref, vt_ref if tail else None)):
            zc = z(c)
            o_ref[rows, :] = zc.astype(o_ref.dtype)
            t0 = r * rc - (tm - KV_WIN)
            if tail and t0 >= 0:
                t_ref[t0:t0 + rc, :] = zc


def _inproj(h, gpre, w_in, gva, *, tm, va_dtype, kv_dtype, tail):
    m = h.shape[0]
    rc = min(tm, PROJ_ROW_CHUNK)
    row_blk = lambda i: (i, 0)
    const = lambda i: (0, 0)
    out_specs = [pl.BlockSpec((tm, D_A), row_blk)] * 5
    out_shape = [
        jax.ShapeDtypeStruct((m, D_A), F32),
        jax.ShapeDtypeStruct((m, D_A), va_dtype),
        jax.ShapeDtypeStruct((m, D_B), BF16),
        jax.ShapeDtypeStruct((m, D_B), kv_dtype),
        jax.ShapeDtypeStruct((m, D_B), kv_dtype),
    ]
    if tail:
        assert tm >= KV_WIN and (tm - KV_WIN) % rc == 0
        out_specs += [pl.BlockSpec((KV_WIN, D_B), const)] * 2
        out_shape += [jax.ShapeDtypeStruct((KV_WIN, D_B), F32)] * 2
    return pl.pallas_call(
        functools.partial(_inproj_kernel, rc=rc, tail=tail),
        grid=(m // tm,),
        in_specs=[
            pl.BlockSpec((tm, D_MODEL), row_blk),
            pl.BlockSpec((1, D_MODEL), const),
            pl.BlockSpec((D_MODEL, D_IN), const, pipeline_mode=pl.Buffered(1)),
            pl.BlockSpec((1, D_A), const),
        ],
        out_specs=out_specs,
        out_shape=out_shape,
        scratch_shapes=[pltpu.VMEM((tm, D_MODEL), BF16)],
        compiler_params=_params(("arbitrary",)),
        name="inproj",
    )(h, gpre, w_in, gva)


def _mask_spatial_weights(sw_ref, wm_ref):
    t = lax.broadcasted_iota(jnp.int32, (MLP_CHUNK, MLP_CHUNK), 0)
    s = lax.broadcasted_iota(jnp.int32, (MLP_CHUNK, MLP_CHUNK), 1)
    for g in range(GROUPS_A):
        wm_ref[g] = jnp.where(s <= t, sw_ref[g], 0.0).astype(BF16)


def _attend(qh, kh, vh, bias, valid):
    s = lax.dot_general(qh, kh, (((1,), (1,)), ((), ())), preferred_element_type=F32) * SCALE + bias
    if valid is not None:
        s = jnp.where(valid, s, NEG_INF)
    e = jnp.exp(s - jnp.max(s, axis=-1, keepdims=True))
    p = e / jnp.sum(e, axis=-1, keepdims=True)
    return jnp.dot(p.astype(BF16), vh, preferred_element_type=F32)


def _gmlp_block(u_ref, va_ref, wm_ref, sbb_ref, goa_ref, cat_ref, a_scr):
    nchunk = u_ref.shape[0] // MLP_CHUNK
    chunk_rows = [slice(n * MLP_CHUNK, (n + 1) * MLP_CHUNK) for n in range(nchunk)]
    for g in range(GROUPS_A):
        cols = slice(g * GROUP_A, (g + 1) * GROUP_A)
        v_all = jnp.concatenate([va_ref[rows, cols] for rows in chunk_rows], axis=1)
        mixed = jnp.dot(wm_ref[g], v_all, preferred_element_type=F32)
        for n, rows in enumerate(chunk_rows):
            a_scr[rows, cols] = u_ref[rows, cols] * (mixed[:, n * GROUP_A:(n + 1) * GROUP_A] + sbb_ref[g])
    for rows in chunk_rows:
        cat_ref[rows, 0:D_A] = _rms(a_scr[rows, :], goa_ref[...]).astype(BF16)


def _attn_groups(q_ref, kbuf, vbuf, bias_ref, gob_ref, cat_ref, o_scr, *, first):
    tb = q_ref.shape[0]
    for gq in range(tb // GROUP_Q):
        q0 = gq * GROUP_Q
        qrows = slice(q0, q0 + GROUP_Q)
        krows = slice(q0, q0 + GROUP_K)
        if first:
            valid = lax.broadcasted_iota(jnp.int32, (GROUP_Q, GROUP_K), 1) >= KV_WIN - q0
        for h in range(N_HEADS):
            cols = slice(h * HEAD_DIM, (h + 1) * HEAD_DIM)
            s2 = lax.dot_general(q_ref[qrows, cols], kbuf[krows, cols], (((1,), (1,)), ((), ())),
                                 preferred_element_type=F32) * (SCALE * LOG2E) + bias_ref[h]
            if first:
                s2 = jnp.where(valid, s2, NEG_INF)
            e = jnp.exp2(s2 - jnp.max(s2, axis=-1, keepdims=True)).astype(BF16)
            nd = jnp.dot(e, vbuf[krows, 2 * h * HEAD_DIM:2 * (h + 1) * HEAD_DIM], preferred_element_type=F32)
            o_scr[:, cols] = nd[:, 0:HEAD_DIM] * (1.0 / nd[:, HEAD_DIM:2 * HEAD_DIM])
        cat_ref[qrows, D_A:D_A + D_B] = _rms(o_scr[...], gob_ref[...]).astype(BF16)


def _mixer_prompt_kernel(u_ref, va_ref, q_ref, k_ref, v_ref, sw_ref, sbb_ref, bias_ref, goa_ref, gob_ref,
                         cat_ref, wm_ref, kbuf, vbuf, a_scr, o_scr):
    i = pl.program_id(0)
    tb = u_ref.shape[0]

    @pl.when(i == 0)
    def _():
        _mask_spatial_weights(sw_ref, wm_ref)
        kbuf[0:KV_WIN, :] = jnp.zeros((KV_WIN, D_B), BF16)
        for h in range(N_HEADS):
            lo = 2 * h * HEAD_DIM
            vbuf[0:KV_WIN, lo:lo + HEAD_DIM] = jnp.zeros((KV_WIN, HEAD_DIM), BF16)
            vbuf[:, lo + HEAD_DIM:lo + 2 * HEAD_DIM] = jnp.ones((KV_WIN + tb, HEAD_DIM), BF16)

    @pl.when(i > 0)
    def _():
        kbuf[0:KV_WIN, :] = kbuf[tb:tb + KV_WIN, :]
        for h in range(N_HEADS):
            lo = 2 * h * HEAD_DIM
            vbuf[0:KV_WIN, lo:lo + HEAD_DIM] = vbuf[tb:tb + KV_WIN, lo:lo + HEAD_DIM]

    kbuf[KV_WIN:KV_WIN + tb, :] = k_ref[...]
    for h in range(N_HEADS):
        vbuf[KV_WIN:KV_WIN + tb, 2 * h * HEAD_DIM:(2 * h + 1) * HEAD_DIM] = v_ref[:, h * HEAD_DIM:(h + 1) * HEAD_DIM]

    _gmlp_block(u_ref, va_ref, wm_ref, sbb_ref, goa_ref, cat_ref, a_scr)

    @pl.when(i == 0)
    def _():
        _attn_groups(q_ref, kbuf, vbuf, bias_ref, gob_ref, cat_ref, o_scr, first=True)

    @pl.when(i > 0)
    def _():
        _attn_groups(q_ref, kbuf, vbuf, bias_ref, gob_ref, cat_ref, o_scr, first=False)


def _mixer_prompt(u, va, q, k, v, sw, sbb, bias, goa, gob, *, tb):
    m = u.shape[0]
    assert tb >= KV_WIN and tb % GROUP_Q == 0
    row_blk = lambda i: (i, 0)
    const2 = lambda i: (0, 0)
    const3 = lambda i: (0, 0, 0)
    return pl.pallas_call(
        _mixer_prompt_kernel,
        grid=(m // tb,),
        in_specs=[
            pl.BlockSpec((tb, D_A), row_blk),
            pl.BlockSpec((tb, D_A), row_blk),
            pl.BlockSpec((tb, D_B), row_blk),
            pl.BlockSpec((tb, D_B), row_blk),
            pl.BlockSpec((tb, D_B), row_blk),
            pl.BlockSpec((GROUPS_A, MLP_CHUNK, MLP_CHUNK), const3),
            pl.BlockSpec((GROUPS_A, MLP_CHUNK, GROUP_A), const3),
            pl.BlockSpec((N_HEADS, GROUP_Q, GROUP_K), const3),
            pl.BlockSpec((1, D_A), const2),
            pl.BlockSpec((1, D_B), const2),
        ],
        out_specs=pl.BlockSpec((tb, D_MODEL), row_blk),
        out_shape=jax.ShapeDtypeStruct((m, D_MODEL), BF16),
        scratch_shapes=[
            pltpu.VMEM((GROUPS_A, MLP_CHUNK, MLP_CHUNK), BF16),
            pltpu.VMEM((KV_WIN + tb, D_B), BF16),
            pltpu.VMEM((KV_WIN + tb, 2 * D_B), BF16),
            pltpu.VMEM((tb, D_A), F32),
            pltpu.VMEM((GROUP_Q, D_B), F32),
        ],
        compiler_params=_params(("arbitrary",)),
        name="mixer_prompt",
    )(u, va, q, k, v, sw, sbb, bias, goa, gob)


def _mixer_sample_kernel(u_ref, va_ref, q_ref, k_ref, v_ref, ck_ref, cv_ref, sw_ref, sbb_ref, bias_ref,
                         goa_ref, gob_ref, cat_ref, wm_ref, vap, kbuf, vbuf, a_scr, o_scr):
    b = pl.program_id(0)
    n = u_ref.shape[0]
    w = ck_ref.shape[1] // N_HEADS

    @pl.when(b == 0)
    def _():
        _mask_spatial_weights(sw_ref, wm_ref)
        vap[...] = jnp.zeros(vap.shape, BF16)

    vap[0:n, :] = va_ref[...].astype(BF16)
    for h in range(N_HEADS):
        cols = slice(h * HEAD_DIM, (h + 1) * HEAD_DIM)
        kbuf[0:w, cols] = ck_ref[0, pl.ds(h, w, stride=N_HEADS), :].astype(BF16)
        vbuf[0:w, cols] = cv_ref[0, pl.ds(h, w, stride=N_HEADS), :].astype(BF16)
    kbuf[w:w + n, :] = k_ref[...].astype(BF16)
    vbuf[w:w + n, :] = v_ref[...].astype(BF16)

    ssq = jnp.zeros((n, 1), F32)
    for g in range(GROUPS_A):
        cols = slice(g * GROUP_A, (g + 1) * GROUP_A)
        mixed = jnp.dot(wm_ref[g, 0:n, :], vap[:, cols], preferred_element_type=F32) + sbb_ref[g, 0:n, :]
        a = u_ref[:, cols] * mixed
        a_scr[:, cols] = a
        ssq = ssq + jnp.sum(a * a, axis=-1, keepdims=True)
    inv = lax.rsqrt(ssq * (1.0 / D_A) + EPS)
    cat_ref[:, 0:D_A] = (a_scr[...] * inv * goa_ref[...]).astype(BF16)

    ssq = jnp.zeros((n, 1), F32)
    for h in range(N_HEADS):
        cols = slice(h * HEAD_DIM, (h + 1) * HEAD_DIM)
        o = _attend(q_ref[:, cols], kbuf[:, cols], vbuf[:, cols], bias_ref[h], None)
        o_scr[:, cols] = o
        ssq = ssq + jnp.sum(o * o, axis=-1, keepdims=True)
    inv = lax.rsqrt(ssq * (1.0 / D_B) + EPS)
    cat_ref[:, D_A:D_A + D_B] = (o_scr[...] * inv * gob_ref[...]).astype(BF16)


def _mixer_sample(u, va, q, k, v, ck, cv, sw, sbb, bias, goa, gob, *, n):
    m = u.shape[0]
    nb, wh, _ = ck.shape
    w = wh // N_HEADS
    row_blk = lambda b: (b, 0)
    const2 = lambda b: (0, 0)
    const3 = lambda b: (0, 0, 0)
    return pl.pallas_call(
        _mixer_sample_kernel,
        grid=(nb,),
        in_specs=[
            pl.BlockSpec((n, D_A), row_blk),
            pl.BlockSpec((n, D_A), row_blk),
            pl.BlockSpec((n, D_B), row_blk),
            pl.BlockSpec((n, D_B), row_blk),
            pl.BlockSpec((n, D_B), row_blk),
            pl.BlockSpec((1, wh, HEAD_DIM), lambda b: (b, 0, 0)),
            pl.BlockSpec((1, wh, HEAD_DIM), lambda b: (b, 0, 0)),
            pl.BlockSpec((GROUPS_A, MLP_CHUNK, MLP_CHUNK), const3),
            pl.BlockSpec((GROUPS_A, MLP_CHUNK, GROUP_A), const3),
            pl.BlockSpec((N_HEADS, n, w + n), const3),
            pl.BlockSpec((1, D_A), const2),
            pl.BlockSpec((1, D_B), const2),
        ],
        out_specs=pl.BlockSpec((n, D_MODEL), row_blk),
        out_shape=jax.ShapeDtypeStruct((m, D_MODEL), BF16),
        scratch_shapes=[
            pltpu.VMEM((GROUPS_A, MLP_CHUNK, MLP_CHUNK), BF16),
            pltpu.VMEM((MLP_CHUNK, D_A), BF16),
            pltpu.VMEM((w + n, D_B), BF16),
            pltpu.VMEM((w + n, D_B), BF16),
            pltpu.VMEM((n, D_A), F32),
            pltpu.VMEM((n, D_B), F32),
        ],
        compiler_params=_params(("arbitrary",)),
        name="mixer_sample",
    )(u, va, q, k, v, ck, cv, sw, sbb, bias, goa, gob)


def _outproj_kernel(cat_ref, h_ref, w_ref, g_ref, o_ref, *, rc):
    for r in range(cat_ref.shape[0] // rc):
        rows = slice(r * rc, (r + 1) * rc)
        m = jnp.dot(cat_ref[rows, :], w_ref[...], preferred_element_type=F32)
        o_ref[rows, :] = h_ref[rows, :] + _rms(m, g_ref[...])


def _outproj(cat, h, w_out, g, *, tm):
    m = cat.shape[0]
    return pl.pallas_call(
        functools.partial(_outproj_kernel, rc=min(tm, PROJ_ROW_CHUNK)),
        grid=(m // tm,),
        in_specs=[
            pl.BlockSpec((tm, D_MODEL), lambda i: (i, 0)),
            pl.BlockSpec((tm, D_MODEL), lambda i: (i, 0)),
            pl.BlockSpec((D_MODEL, D_MODEL), lambda i: (0, 0), pipeline_mode=pl.Buffered(1)),
            pl.BlockSpec((1, D_MODEL), lambda i: (0, 0)),
        ],
        out_specs=pl.BlockSpec((tm, D_MODEL), lambda i: (i, 0)),
        out_shape=jax.ShapeDtypeStruct((m, D_MODEL), F32),
        compiler_params=_params(("parallel",)),
        name="outproj",
    )(cat, h, w_out, g)


def _rel_bias_table(rel_bias, lq, lk, band_chunks=None):
    nh = rel_bias.shape[0]
    p = lq + lk - 1
    t_min, t_max = KV_WIN - lk + 1, KV_WIN + lq - 1
    n_lo = max(0, -REL_CLIP - t_min)
    n_hi = max(0, t_max - REL_CLIP)
    mid = rel_bias[:, max(t_min, -REL_CLIP) + REL_CLIP:min(t_max, REL_CLIP) + REL_CLIP + 1]
    f = jnp.concatenate([jnp.broadcast_to(rel_bias[:, :1], (nh, n_lo)), mid,
                         jnp.broadcast_to(rel_bias[:, -1:], (nh, n_hi))], axis=1).astype(F32)
    assert f.shape == (nh, p)
    w = jnp.concatenate([f[:, ::-1], jnp.zeros((nh, 1), F32)], axis=1)
    skew = jnp.broadcast_to(w[:, None, :], (nh, lq, p + 1)).reshape(nh, lq * (p + 1))[:, :lq * p]
    table = skew.reshape(nh, lq, p)[:, :, lq - 1:lq - 1 + lk]
    if band_chunks is not None:
        qc = jnp.arange(lq)[:, None] // CHUNK
        jc = jnp.arange(lk)[None, :] // CHUNK
        table = jnp.where((jc >= qc) & (jc <= qc + band_chunks), table, NEG_INF)
    return table


def kernel(x_prompt, x_sample, cache_b_k, cache_b_v, w_in, w_out, sw_a, sb_a, rel_bias, g_va, g_oa, g_ob,
           g_f1_pre, g_f1_post, g_m_pre, g_m_post, g_f2_pre, g_f2_post,
           f1_gate, f1_up, f1_down, f2_gate, f2_up, f2_down):
    depth = w_in.shape[0]
    assert depth == 1
    bp, sp, _ = x_prompt.shape
    bs, ns, _ = x_sample.shape
    assert bp == 1
    l = 0
    row = lambda g: g[l].reshape(1, -1)
    w_in_b = w_in[l].astype(BF16)
    w_out_b = w_out[l].astype(BF16)
    sbb = jnp.broadcast_to(sb_a[l][:, :, None], (GROUPS_A, MLP_CHUNK, GROUP_A))
    bias_p = _rel_bias_table(rel_bias[l], GROUP_Q, GROUP_K, band_chunks=LEFT_CHUNKS) * LOG2E
    wc = cache_b_k.shape[2]
    bias_s = _rel_bias_table(rel_bias[l], ns, wc + ns)

    xp = x_prompt.reshape(sp, D_MODEL)
    ms = bs * ns
    xs = x_sample.reshape(ms, D_MODEL)
    ffn_tiles = dict(tm=FFN_ROW_TILE, tf=FFN_HIDDEN_TILE)

    h_head, hs, *f1 = _ffn_head(xp, xs, row(g_f1_pre), row(g_f1_post), f1_gate[l], f1_up[l], f1_down[l],
                                tm=FFN_ROW_TILE, tf=FFN_HEAD_HIDDEN_TILE)
    h = _ffn(xp, row(g_f1_pre), row(g_f1_post), *f1, h_head, **ffn_tiles)

    u, va, q, k, v, kt, vt = _inproj(h, row(g_m_pre), w_in_b, row(g_va), tm=512,
                                     va_dtype=BF16, kv_dtype=BF16, tail=True)
    cat = _mixer_prompt(u, va, q, k, v, sw_a[l], sbb, bias_p, row(g_oa), row(g_ob), tb=512)
    h2 = _outproj(cat, h, w_out_b, row(g_m_post), tm=512)

    us, vas, qs, ks, vs = _inproj(hs, row(g_m_pre), w_in_b, row(g_va), tm=ms,
                                  va_dtype=F32, kv_dtype=F32, tail=False)
    ck = cache_b_k[l].reshape(bs, wc * N_HEADS, HEAD_DIM)
    cv = cache_b_v[l].reshape(bs, wc * N_HEADS, HEAD_DIM)
    cats = _mixer_sample(us, vas, qs, ks, vs, ck, cv, sw_a[l], sbb, bias_s, row(g_oa), row(g_ob), n=ns)
    hs2 = _outproj(cats, hs, w_out_b, row(g_m_post), tm=ms)

    y_head, ys, *f2 = _ffn_head(h2, hs2, row(g_f2_pre), row(g_f2_post), f2_gate[l], f2_up[l], f2_down[l],
                                tm=FFN_ROW_TILE, tf=FFN_HEAD_HIDDEN_TILE)
    yp = _ffn(h2, row(g_f2_pre), row(g_f2_post), *f2, y_head, **ffn_tiles)

    hd = (N_HEADS, HEAD_DIM)
    return (
        yp.reshape(bp, sp, D_MODEL),
        ys.reshape(bs, ns, D_MODEL),
        kt.reshape(depth, bp, KV_WIN, *hd),
        vt.reshape(depth, bp, KV_WIN, *hd),
        ks.reshape(depth, bs, ns, *hd),
        vs.reshape(depth, bs, ns, *hd),
        vas.reshape(depth, bs, ns, D_A),
    )
```

```python
import functools

import jax
import jax.numpy as jnp
from jax import lax
from jax.experimental import pallas as pl
from jax.experimental.pallas import tpu as pltpu

D_MODEL = 2048
D_A = 1024
D_B = 1024
GROUPS_A = 8
GROUP_A = 128
MLP_CHUNK = 128
N_HEADS = 8
HEAD_DIM = 128
CHUNK = 64
LEFT_CHUNKS = 8
KV_WIN = LEFT_CHUNKS * CHUNK
BAND = (LEFT_CHUNKS + 1) * CHUNK
GROUP_CHUNKS = 4
GROUP_Q = GROUP_CHUNKS * CHUNK
GROUP_K = GROUP_Q + KV_WIN
REL_CLIP = 128
D_FF = 5632
D_IN = 2 * D_A + 3 * D_B
EPS = 1e-6
NEG_INF = -1e30
SCALE = HEAD_DIM ** -0.5
LOG2E = 1.4426950408889634

F32 = jnp.float32
BF16 = jnp.bfloat16
LANES = 128

VMEM_LIMIT_BYTES = 60 * 1024 * 1024
FFN_ROW_CHUNK = 256
PROJ_ROW_CHUNK = 256
PROJ_ROW_TILE = 512
MIXER_ROW_TILE = 512
FFN_ROW_TILE = 1024
FFN_HIDDEN_TILE = 512
FFN_HEAD_HIDDEN_TILE = 256
CAST_SLABS = 64


def _rms(x, g):
    return x * lax.rsqrt(jnp.mean(x * x, axis=-1, keepdims=True) + EPS) * g


def _params(sem):
    return pltpu.CompilerParams(dimension_semantics=sem, vmem_limit_bytes=VMEM_LIMIT_BYTES)


def _ffn_rows(x_ref, o_ref, xn_ref, gpre_ref, gpost_ref, wg_ref, wu_ref, wd_ref, *, rc, first, last):
    for r in range(x_ref.shape[0] // rc):
        rows = slice(r * rc, (r + 1) * rc)
        if first:
            xn_ref[rows, :] = _rms(x_ref[rows, :], gpre_ref[...]).astype(BF16)
        xn = xn_ref[rows, :]
        parts = []
        for t in range(wg_ref.shape[0]):
            g = jnp.dot(xn, wg_ref[t], preferred_element_type=F32)
            u = jnp.dot(xn, wu_ref[t], preferred_element_type=F32)
            parts.append((g * jax.nn.sigmoid(g) * u).astype(BF16))
        a = parts[0] if len(parts) == 1 else jnp.concatenate(parts, axis=1)
        d = jnp.dot(a, wd_ref[...], preferred_element_type=F32)
        if first:
            o_ref[rows, :] = d
        elif last:
            o_ref[rows, :] = x_ref[rows, :] + 0.5 * _rms(o_ref[rows, :] + d, gpost_ref[...])
        else:
            o_ref[rows, :] += d


def _when_first_mid_last(j, nj, enabled, body):
    pl.when(jnp.logical_and(enabled, j == 0))(functools.partial(body, first=True, last=False))
    pl.when(jnp.logical_and(enabled, jnp.logical_and(j > 0, j < nj - 1)))(
        functools.partial(body, first=False, last=False))
    pl.when(jnp.logical_and(enabled, j == nj - 1))(functools.partial(body, first=False, last=True))


def _ffn_head_kernel(xp_ref, xs_ref, gpre_ref, gpost_ref, wgf_ref, wuf_ref, wdf_ref,
                     op_ref, os_ref, wg_ref, wu_ref, wd_ref, xnp_ref, xns_ref, *, rc):
    def body(first, last):
        wg_ref[0] = wgf_ref[...].astype(BF16)
        wu_ref[0] = wuf_ref[...].astype(BF16)
        wd_ref[...] = wdf_ref[...].astype(BF16)
        for x_ref, o_ref, xn_ref in ((xp_ref, op_ref, xnp_ref), (xs_ref, os_ref, xns_ref)):
            _ffn_rows(x_ref, o_ref, xn_ref, gpre_ref, gpost_ref, wg_ref, wu_ref, wd_ref,
                      rc=min(rc, x_ref.shape[0]), first=first, last=last)

    _when_first_mid_last(pl.program_id(0), pl.num_programs(0), True, body)


def _ffn_head(xp, xs, gpre, gpost, wg, wu, wd, *, tm, tf):
    nj = D_FF // tf
    ms = xs.shape[0]
    assert nj >= 2
    const = lambda j: (0, 0)
    vec = pl.BlockSpec((1, D_MODEL), const)
    return pl.pallas_call(
        functools.partial(_ffn_head_kernel, rc=FFN_ROW_CHUNK),
        grid=(nj,),
        in_specs=[
            pl.BlockSpec((tm, D_MODEL), const, pipeline_mode=pl.Buffered(1)),
            pl.BlockSpec((ms, D_MODEL), const, pipeline_mode=pl.Buffered(1)),
            vec, vec,
            pl.BlockSpec((D_MODEL, tf), lambda j: (0, j)),
            pl.BlockSpec((D_MODEL, tf), lambda j: (0, j)),
            pl.BlockSpec((tf, D_MODEL), lambda j: (j, 0)),
        ],
        out_specs=[
            pl.BlockSpec((tm, D_MODEL), const),
            pl.BlockSpec((ms, D_MODEL), const),
            pl.BlockSpec((1, D_MODEL, tf), lambda j: (j, 0, 0)),
            pl.BlockSpec((1, D_MODEL, tf), lambda j: (j, 0, 0)),
            pl.BlockSpec((tf, D_MODEL), lambda j: (j, 0)),
        ],
        out_shape=[
            jax.ShapeDtypeStruct((tm, D_MODEL), F32),
            jax.ShapeDtypeStruct((ms, D_MODEL), F32),
            jax.ShapeDtypeStruct((nj, D_MODEL, tf), BF16),
            jax.ShapeDtypeStruct((nj, D_MODEL, tf), BF16),
            jax.ShapeDtypeStruct((D_FF, D_MODEL), BF16),
        ],
        scratch_shapes=[pltpu.VMEM((tm, D_MODEL), BF16), pltpu.VMEM((ms, D_MODEL), BF16)],
        compiler_params=_params(("arbitrary",)),
        name="ffn_head",
    )(xp, xs, gpre, gpost, wg, wu, wd)


def _ffn_kernel(x_ref, gpre_ref, gpost_ref, wg_ref, wu_ref, wd_ref, head_ref, *rest, rc, n_cast):
    cast_src, o_ref, cast_dst, xn_ref = rest[:n_cast], rest[n_cast], rest[n_cast + 1:-1], rest[-1]
    i = pl.program_id(0)
    j = pl.program_id(1)

    def body(first, last):
        for src, dst in zip(cast_src, cast_dst, strict=True):
            dst[...] = src[...].astype(BF16)
        _ffn_rows(x_ref, o_ref, xn_ref, gpre_ref, gpost_ref, wg_ref, wu_ref, wd_ref,
                  rc=rc, first=first, last=last)

    _when_first_mid_last(j, pl.num_programs(1), i > 0, body)

    @pl.when(jnp.logical_and(i == 0, j == 0))
    def _():
        pltpu.sync_copy(head_ref, o_ref)


def _ffn(x, gpre, gpost, wg, wu, wd, head, *, tm, tf, cast=()):
    m = x.shape[0]
    sub = tf // wg.shape[2]
    nj = wg.shape[0] // sub
    assert head.shape == (tm, D_MODEL) and m // tm >= 2 and nj >= 2
    assert (m // tm - 1) * nj >= CAST_SLABS
    hid = lambda i, j: jnp.where(i == 0, 0, j)
    slab = lambda i, j: (jnp.clip((i - 1) * nj + j, 0, CAST_SLABS - 1), 0)
    slab_spec = lambda w: pl.BlockSpec((w.shape[0] // CAST_SLABS, w.shape[1]), slab)
    vec = pl.BlockSpec((1, D_MODEL), lambda i, j: (0, 0))
    outs = pl.pallas_call(
        functools.partial(_ffn_kernel, rc=FFN_ROW_CHUNK, n_cast=len(cast)),
        grid=(m // tm, nj),
        in_specs=[
            pl.BlockSpec((tm, D_MODEL), lambda i, j: (jnp.maximum(i, 1), 0)), vec, vec,
            pl.BlockSpec((sub, D_MODEL, tf // sub), lambda i, j: (hid(i, j), 0, 0)),
            pl.BlockSpec((sub, D_MODEL, tf // sub), lambda i, j: (hid(i, j), 0, 0)),
            pl.BlockSpec((tf, D_MODEL), lambda i, j: (hid(i, j), 0)),
            pl.BlockSpec(memory_space=pl.ANY),
        ] + [slab_spec(w) for w in cast],
        out_specs=[pl.BlockSpec((tm, D_MODEL), lambda i, j: (i, 0))] + [slab_spec(w) for w in cast],
        out_shape=[jax.ShapeDtypeStruct((m, D_MODEL), F32)]
        + [jax.ShapeDtypeStruct(w.shape, BF16) for w in cast],
        scratch_shapes=[pltpu.VMEM((tm, D_MODEL), BF16)],
        compiler_params=_params(("arbitrary", "arbitrary")),
        name="ffn",
    )(x, gpre, gpost, wg, wu, wd, head, *cast)
    return outs if cast else outs[0]


def _inproj_rows(h_ref, hn_ref, gpre_ref, w_ref, gva_ref, u_ref, va_ref, q_ref, k_ref, v_ref, tails, rc):
    tm = h_ref.shape[0]
    for r in range(tm // rc):
        rows = slice(r * rc, (r + 1) * rc)
        hn_ref[rows, :] = _rms(h_ref[rows, :], gpre_ref[...]).astype(BF16)
        hn = hn_ref[rows, :]
        z = lambda c: jnp.dot(hn, w_ref[:, c * D_A:(c + 1) * D_A], preferred_element_type=F32)
        u_ref[rows, :] = jax.nn.gelu(z(0))
        va_ref[rows, :] = _rms(jax.nn.gelu(z(1)), gva_ref[...]).astype(va_ref.dtype)
        q_ref[rows, :] = z(2).astype(q_ref.dtype)
        for n, (c, o_ref) in enumerate(((3, k_ref), (4, v_ref))):
            zc = z(c)
            o_ref[rows, :] = zc.astype(o_ref.dtype)
            t0 = r * rc - (tm - KV_WIN)
            if tails is not None and t0 >= 0:
                tails[n][t0:t0 + rc, :] = zc


def _inproj_kernel(hp_ref, hs_ref, gpre_ref, w_ref, gva_ref,
                   up_ref, vap_ref, qp_ref, kp_ref, vp_ref, kt_ref, vt_ref,
                   us_ref, vas_ref, qs_ref, ks_ref, vs_ref, hnp_ref, hns_ref, *, rc):
    _inproj_rows(hp_ref, hnp_ref, gpre_ref, w_ref, gva_ref, up_ref, vap_ref, qp_ref, kp_ref, vp_ref,
                 (kt_ref, vt_ref), rc)

    @pl.when(pl.program_id(0) == 0)
    def _():
        _inproj_rows(hs_ref, hns_ref, gpre_ref, w_ref, gva_ref, us_ref, vas_ref, qs_ref, ks_ref, vs_ref,
                     None, min(rc, hs_ref.shape[0]))


def _inproj(hp, hs, gpre, w_in, gva, *, tm):
    m, ms = hp.shape[0], hs.shape[0]
    rc = min(tm, PROJ_ROW_CHUNK)
    assert tm >= KV_WIN and (tm - KV_WIN) % rc == 0
    row_blk = lambda i: (i, 0)
    const = lambda i: (0, 0)
    p_dtypes = (F32, BF16, BF16, BF16, BF16)
    s_dtypes = (F32, F32, BF16, F32, F32)
    return pl.pallas_call(
        functools.partial(_inproj_kernel, rc=rc),
        grid=(m // tm,),
        in_specs=[
            pl.BlockSpec((tm, D_MODEL), row_blk),
            pl.BlockSpec((ms, D_MODEL), const),
            pl.BlockSpec((1, D_MODEL), const),
            pl.BlockSpec((D_MODEL, D_IN), const, pipeline_mode=pl.Buffered(1)),
            pl.BlockSpec((1, D_A), const),
        ],
        out_specs=[pl.BlockSpec((tm, D_A), row_blk)] * 5 + [pl.BlockSpec((KV_WIN, D_B), const)] * 2
        + [pl.BlockSpec((ms, D_A), const)] * 5,
        out_shape=[jax.ShapeDtypeStruct((m, D_A), d) for d in p_dtypes]
        + [jax.ShapeDtypeStruct((KV_WIN, D_B), F32)] * 2
        + [jax.ShapeDtypeStruct((ms, D_A), d) for d in s_dtypes],
        scratch_shapes=[pltpu.VMEM((tm, D_MODEL), BF16), pltpu.VMEM((ms, D_MODEL), BF16)],
        compiler_params=_params(("arbitrary",)),
        name="inproj",
    )(hp, hs, gpre, w_in, gva)


def _mask_spatial_weights(sw_ref, wm_ref):
    t = lax.broadcasted_iota(jnp.int32, (MLP_CHUNK, MLP_CHUNK), 0)
    s = lax.broadcasted_iota(jnp.int32, (MLP_CHUNK, MLP_CHUNK), 1)
    for g in range(GROUPS_A):
        wm_ref[g] = jnp.where(s <= t, sw_ref[g], 0.0).astype(BF16)


def _attend(qh, kh, vh, bias, valid):
    s = lax.dot_general(qh, kh, (((1,), (1,)), ((), ())), preferred_element_type=F32) * SCALE + bias
    if valid is not None:
        s = jnp.where(valid, s, NEG_INF)
    e = jnp.exp(s - jnp.max(s, axis=-1, keepdims=True))
    p = e / jnp.sum(e, axis=-1, keepdims=True)
    return jnp.dot(p.astype(BF16), vh, preferred_element_type=F32)


def _gmlp_block(u_ref, va_ref, wm_ref, sbb_ref, goa_ref, cat_ref, a_scr):
    nchunk = u_ref.shape[0] // MLP_CHUNK
    chunk_rows = [slice(n * MLP_CHUNK, (n + 1) * MLP_CHUNK) for n in range(nchunk)]
    for g in range(GROUPS_A):
        cols = slice(g * GROUP_A, (g + 1) * GROUP_A)
        v_all = jnp.concatenate([va_ref[rows, cols] for rows in chunk_rows], axis=1)
        mixed = jnp.dot(wm_ref[g], v_all, preferred_element_type=F32)
        for n, rows in enumerate(chunk_rows):
            a_scr[rows, cols] = u_ref[rows, cols] * (mixed[:, n * GROUP_A:(n + 1) * GROUP_A] + sbb_ref[g])
    for rows in chunk_rows:
        cat_ref[rows, 0:D_A] = _rms(a_scr[rows, :], goa_ref[...]).astype(BF16)


def _attn_groups(q_ref, kbuf, vbuf, bias_ref, gob_ref, cat_ref, o_scr, *, first):
    tb = q_ref.shape[0]
    for gq in range(tb // GROUP_Q):
        q0 = gq * GROUP_Q
        qrows = slice(q0, q0 + GROUP_Q)
        krows = slice(q0, q0 + GROUP_K)
        if first:
            valid = lax.broadcasted_iota(jnp.int32, (GROUP_Q, GROUP_K), 1) >= KV_WIN - q0
        for h in range(N_HEADS):
            cols = slice(h * HEAD_DIM, (h + 1) * HEAD_DIM)
            s2 = lax.dot_general(q_ref[qrows, cols], kbuf[krows, cols], (((1,), (1,)), ((), ())),
                                 preferred_element_type=F32) * (SCALE * LOG2E) + bias_ref[h]
            if first:
                s2 = jnp.where(valid, s2, NEG_INF)
            e = jnp.exp2(s2 - jnp.max(s2, axis=-1, keepdims=True)).astype(BF16)
            nd = jnp.dot(e, vbuf[krows, 2 * h * HEAD_DIM:2 * (h + 1) * HEAD_DIM], preferred_element_type=F32)
            o_scr[:, cols] = nd[:, 0:HEAD_DIM] * (1.0 / nd[:, HEAD_DIM:2 * HEAD_DIM])
        cat_ref[qrows, D_A:D_A + D_B] = _rms(o_scr[...], gob_ref[...]).astype(BF16)


def _expand_rel_bias(relw_row, lq, lk):
    b = jnp.broadcast_to(relw_row, (lq, relw_row.shape[1]))
    return pltpu.roll(b, 0, 1, stride=1, stride_axis=0)[:, 0:lk]


def _mixer_prompt_kernel(u_ref, va_ref, q_ref, k_ref, v_ref, sw_ref, sbb_ref, relw_ref, goa_ref, gob_ref,
                         cat_ref, wm_ref, kbuf, vbuf, a_scr, o_scr, bias_ref):
    i = pl.program_id(0)
    tb = u_ref.shape[0]

    @pl.when(i == 0)
    def _():
        _mask_spatial_weights(sw_ref, wm_ref)
        qc = lax.broadcasted_iota(jnp.int32, (GROUP_Q, GROUP_K), 0) // CHUNK
        jc = lax.broadcasted_iota(jnp.int32, (GROUP_Q, GROUP_K), 1) // CHUNK
        band = jnp.logical_and(jc >= qc, jc <= qc + LEFT_CHUNKS)
        for h in range(N_HEADS):
            bias_ref[h] = jnp.where(band, _expand_rel_bias(relw_ref[h:h + 1, :], GROUP_Q, GROUP_K), NEG_INF)
        kbuf[0:KV_WIN, :] = jnp.zeros((KV_WIN, D_B), BF16)
        for h in range(N_HEADS):
            lo = 2 * h * HEAD_DIM
            vbuf[0:KV_WIN, lo:lo + HEAD_DIM] = jnp.zeros((KV_WIN, HEAD_DIM), BF16)
            vbuf[:, lo + HEAD_DIM:lo + 2 * HEAD_DIM] = jnp.ones((KV_WIN + tb, HEAD_DIM), BF16)

    @pl.when(i > 0)
    def _():
        kbuf[0:KV_WIN, :] = kbuf[tb:tb + KV_WIN, :]
        for h in range(N_HEADS):
            lo = 2 * h * HEAD_DIM
            vbuf[0:KV_WIN, lo:lo + HEAD_DIM] = vbuf[tb:tb + KV_WIN, lo:lo + HEAD_DIM]

    kbuf[KV_WIN:KV_WIN + tb, :] = k_ref[...]
    for h in range(N_HEADS):
        vbuf[KV_WIN:KV_WIN + tb, 2 * h * HEAD_DIM:(2 * h + 1) * HEAD_DIM] = v_ref[:, h * HEAD_DIM:(h + 1) * HEAD_DIM]

    _gmlp_block(u_ref, va_ref, wm_ref, sbb_ref, goa_ref, cat_ref, a_scr)

    @pl.when(i == 0)
    def _():
        _attn_groups(q_ref, kbuf, vbuf, bias_ref, gob_ref, cat_ref, o_scr, first=True)

    @pl.when(i > 0)
    def _():
        _attn_groups(q_ref, kbuf, vbuf, bias_ref, gob_ref, cat_ref, o_scr, first=False)


def _mixer_prompt(u, va, q, k, v, sw, sbb, relw, goa, gob, *, tb):
    m = u.shape[0]
    assert tb >= KV_WIN and tb % GROUP_Q == 0
    row_blk = lambda i: (i, 0)
    const2 = lambda i: (0, 0)
    const3 = lambda i: (0, 0, 0)
    return pl.pallas_call(
        _mixer_prompt_kernel,
        grid=(m // tb,),
        in_specs=[
            pl.BlockSpec((tb, D_A), row_blk),
            pl.BlockSpec((tb, D_A), row_blk),
            pl.BlockSpec((tb, D_B), row_blk),
            pl.BlockSpec((tb, D_B), row_blk),
            pl.BlockSpec((tb, D_B), row_blk),
            pl.BlockSpec((GROUPS_A, MLP_CHUNK, MLP_CHUNK), const3),
            pl.BlockSpec((GROUPS_A, MLP_CHUNK, GROUP_A), const3),
            pl.BlockSpec(relw.shape, const2),
            pl.BlockSpec((1, D_A), const2),
            pl.BlockSpec((1, D_B), const2),
        ],
        out_specs=pl.BlockSpec((tb, D_MODEL), row_blk),
        out_shape=jax.ShapeDtypeStruct((m, D_MODEL), BF16),
        scratch_shapes=[
            pltpu.VMEM((GROUPS_A, MLP_CHUNK, MLP_CHUNK), BF16),
            pltpu.VMEM((KV_WIN + tb, D_B), BF16),
            pltpu.VMEM((KV_WIN + tb, 2 * D_B), BF16),
            pltpu.VMEM((tb, D_A), F32),
            pltpu.VMEM((GROUP_Q, D_B), F32),
            pltpu.VMEM((N_HEADS, GROUP_Q, GROUP_K), F32),
        ],
        compiler_params=_params(("arbitrary",)),
        name="mixer_prompt",
    )(u, va, q, k, v, sw, sbb, relw, goa, gob)


def _mixer_sample_kernel(u_ref, va_ref, q_ref, k_ref, v_ref, ck_ref, cv_ref, sw_ref, sbb_ref, relw_ref,
                         goa_ref, gob_ref, cat_ref, wm_ref, vap, kbuf, vbuf, a_scr, o_scr, bias_ref):
    b = pl.program_id(0)
    n = u_ref.shape[0]
    w = ck_ref.shape[1] // N_HEADS

    @pl.when(b == 0)
    def _():
        _mask_spatial_weights(sw_ref, wm_ref)
        for h in range(N_HEADS):
            bias_ref[h] = _expand_rel_bias(relw_ref[h:h + 1, :], n, w + n)
        vap[...] = jnp.zeros(vap.shape, BF16)

    vap[0:n, :] = va_ref[...].astype(BF16)
    for h in range(N_HEADS):
        cols = slice(h * HEAD_DIM, (h + 1) * HEAD_DIM)
        kbuf[0:w, cols] = ck_ref[0, pl.ds(h, w, stride=N_HEADS), :].astype(BF16)
        vbuf[0:w, cols] = cv_ref[0, pl.ds(h, w, stride=N_HEADS), :].astype(BF16)
    kbuf[w:w + n, :] = k_ref[...].astype(BF16)
    vbuf[w:w + n, :] = v_ref[...].astype(BF16)

    ssq = jnp.zeros((n, 1), F32)
    for g in range(GROUPS_A):
        cols = slice(g * GROUP_A, (g + 1) * GROUP_A)
        mixed = jnp.dot(wm_ref[g, 0:n, :], vap[:, cols], preferred_element_type=F32) + sbb_ref[g, 0:n, :]
        a = u_ref[:, cols] * mixed
        a_scr[:, cols] = a
        ssq = ssq + jnp.sum(a * a, axis=-1, keepdims=True)
    inv = lax.rsqrt(ssq * (1.0 / D_A) + EPS)
    cat_ref[:, 0:D_A] = (a_scr[...] * inv * goa_ref[...]).astype(BF16)

    ssq = jnp.zeros((n, 1), F32)
    for h in range(N_HEADS):
        cols = slice(h * HEAD_DIM, (h + 1) * HEAD_DIM)
        o = _attend(q_ref[:, cols], kbuf[:, cols], vbuf[:, cols], bias_ref[h], None)
        o_scr[:, cols] = o
        ssq = ssq + jnp.sum(o * o, axis=-1, keepdims=True)
    inv = lax.rsqrt(ssq * (1.0 / D_B) + EPS)
    cat_ref[:, D_A:D_A + D_B] = (o_scr[...] * inv * gob_ref[...]).astype(BF16)


def _mixer_sample(u, va, q, k, v, ck, cv, sw, sbb, relw, goa, gob, *, n):
    m = u.shape[0]
    nb, wh, _ = ck.shape
    w = wh // N_HEADS
    row_blk = lambda b: (b, 0)
    const2 = lambda b: (0, 0)
    const3 = lambda b: (0, 0, 0)
    return pl.pallas_call(
        _mixer_sample_kernel,
        grid=(nb,),
        in_specs=[
            pl.BlockSpec((n, D_A), row_blk),
            pl.BlockSpec((n, D_A), row_blk),
            pl.BlockSpec((n, D_B), row_blk),
            pl.BlockSpec((n, D_B), row_blk),
            pl.BlockSpec((n, D_B), row_blk),
            pl.BlockSpec((1, wh, HEAD_DIM), lambda b: (b, 0, 0)),
            pl.BlockSpec((1, wh, HEAD_DIM), lambda b: (b, 0, 0)),
            pl.BlockSpec((GROUPS_A, MLP_CHUNK, MLP_CHUNK), const3),
            pl.BlockSpec((GROUPS_A, MLP_CHUNK, GROUP_A), const3),
            pl.BlockSpec(relw.shape, const2),
            pl.BlockSpec((1, D_A), const2),
            pl.BlockSpec((1, D_B), const2),
        ],
        out_specs=pl.BlockSpec((n, D_MODEL), row_blk),
        out_shape=jax.ShapeDtypeStruct((m, D_MODEL), BF16),
        scratch_shapes=[
            pltpu.VMEM((GROUPS_A, MLP_CHUNK, MLP_CHUNK), BF16),
            pltpu.VMEM((MLP_CHUNK, D_A), BF16),
            pltpu.VMEM((w + n, D_B), BF16),
            pltpu.VMEM((w + n, D_B), BF16),
            pltpu.VMEM((n, D_A), F32),
            pltpu.VMEM((n, D_B), F32),
            pltpu.VMEM((N_HEADS, n, w + n), F32),
        ],
        compiler_params=_params(("arbitrary",)),
        name="mixer_sample",
    )(u, va, q, k, v, ck, cv, sw, sbb, relw, goa, gob)


def _outproj_rows(cat_ref, h_ref, w_ref, g_ref, o_ref, rc):
    for r in range(cat_ref.shape[0] // rc):
        rows = slice(r * rc, (r + 1) * rc)
        m = jnp.dot(cat_ref[rows, :], w_ref[...], preferred_element_type=F32)
        o_ref[rows, :] = h_ref[rows, :] + _rms(m, g_ref[...])


def _outproj_kernel(catp_ref, hp_ref, cats_ref, hs_ref, w_ref, g_ref, op_ref, os_ref, *, rc):
    _outproj_rows(catp_ref, hp_ref, w_ref, g_ref, op_ref, rc)

    @pl.when(pl.program_id(0) == 0)
    def _():
        _outproj_rows(cats_ref, hs_ref, w_ref, g_ref, os_ref, min(rc, cats_ref.shape[0]))


def _outproj(catp, hp, cats, hs, w_out, g, *, tm):
    m, ms = catp.shape[0], cats.shape[0]
    row_blk = lambda i: (i, 0)
    const = lambda i: (0, 0)
    return pl.pallas_call(
        functools.partial(_outproj_kernel, rc=min(tm, PROJ_ROW_CHUNK)),
        grid=(m // tm,),
        in_specs=[
            pl.BlockSpec((tm, D_MODEL), row_blk),
            pl.BlockSpec((tm, D_MODEL), row_blk),
            pl.BlockSpec((ms, D_MODEL), const),
            pl.BlockSpec((ms, D_MODEL), const),
            pl.BlockSpec((D_MODEL, D_MODEL), const, pipeline_mode=pl.Buffered(1)),
            pl.BlockSpec((1, D_MODEL), const),
        ],
        out_specs=[pl.BlockSpec((tm, D_MODEL), row_blk), pl.BlockSpec((ms, D_MODEL), const)],
        out_shape=[jax.ShapeDtypeStruct((m, D_MODEL), F32), jax.ShapeDtypeStruct((ms, D_MODEL), F32)],
        compiler_params=_params(("arbitrary",)),
        name="outproj",
    )(catp, hp, cats, hs, w_out, g)


def _rel_bias_rows(rel_bias, lq, lk):
    nh = rel_bias.shape[0]
    p = lq + lk - 1
    t_min, t_max = KV_WIN - lk + 1, KV_WIN + lq - 1
    n_lo = max(0, -REL_CLIP - t_min)
    n_hi = max(0, t_max - REL_CLIP)
    mid = rel_bias[:, max(t_min, -REL_CLIP) + REL_CLIP:min(t_max, REL_CLIP) + REL_CLIP + 1]
    by_dist = jnp.concatenate([jnp.broadcast_to(rel_bias[:, :1], (nh, n_lo)), mid,
                               jnp.broadcast_to(rel_bias[:, -1:], (nh, n_hi))], axis=1).astype(F32)
    assert by_dist.shape == (nh, p)
    padded = jnp.pad(by_dist[:, ::-1], ((0, 0), (0, -p % LANES)))
    return jnp.roll(padded, -(lq - 1), axis=1)


def kernel(x_prompt, x_sample, cache_b_k, cache_b_v, w_in, w_out, sw_a, sb_a, rel_bias, g_va, g_oa, g_ob,
           g_f1_pre, g_f1_post, g_m_pre, g_m_post, g_f2_pre, g_f2_post,
           f1_gate, f1_up, f1_down, f2_gate, f2_up, f2_down):
    depth = w_in.shape[0]
    assert depth == 1
    bp, sp, _ = x_prompt.shape
    bs, ns, _ = x_sample.shape
    assert bp == 1
    l = 0
    row = lambda g: g[l].reshape(1, -1)
    sbb = jnp.broadcast_to(sb_a[l][:, :, None], (GROUPS_A, MLP_CHUNK, GROUP_A))
    bias_p = _rel_bias_rows(rel_bias[l], GROUP_Q, GROUP_K) * LOG2E
    wc = cache_b_k.shape[2]
    bias_s = _rel_bias_rows(rel_bias[l], ns, wc + ns)

    xp = x_prompt.reshape(sp, D_MODEL)
    ms = bs * ns
    xs = x_sample.reshape(ms, D_MODEL)
    ffn_tiles = dict(tm=FFN_ROW_TILE, tf=FFN_HIDDEN_TILE)

    h_head, hs, *f1 = _ffn_head(xp, xs, row(g_f1_pre), row(g_f1_post), f1_gate[l], f1_up[l], f1_down[l],
                                tm=FFN_ROW_TILE, tf=FFN_HEAD_HIDDEN_TILE)
    h, w_in_b, w_out_b = _ffn(xp, row(g_f1_pre), row(g_f1_post), *f1, h_head, **ffn_tiles,
                              cast=(w_in[l], w_out[l]))

    u, va, q, k, v, kt, vt, us, vas, qs, ks, vs = _inproj(h, hs, row(g_m_pre), w_in_b, row(g_va),
                                                           tm=PROJ_ROW_TILE)
    cat = _mixer_prompt(u, va, q, k, v, sw_a[l], sbb, bias_p, row(g_oa), row(g_ob), tb=MIXER_ROW_TILE)
    ck = cache_b_k[l].reshape(bs, wc * N_HEADS, HEAD_DIM)
    cv = cache_b_v[l].reshape(bs, wc * N_HEADS, HEAD_DIM)
    cats = _mixer_sample(us, vas, qs, ks, vs, ck, cv, sw_a[l], sbb, bias_s, row(g_oa), row(g_ob), n=ns)
    h2, hs2 = _outproj(cat, h, cats, hs, w_out_b, row(g_m_post), tm=PROJ_ROW_TILE)

    y_head, ys, *f2 = _ffn_head(h2, hs2, row(g_f2_pre), row(g_f2_post), f2_gate[l], f2_up[l], f2_down[l],
                                tm=FFN_ROW_TILE, tf=FFN_HEAD_HIDDEN_TILE)
    yp = _ffn(h2, row(g_f2_pre), row(g_f2_post), *f2, y_head, **ffn_tiles)

    hd = (N_HEADS, HEAD_DIM)
    return (
        yp.reshape(bp, sp, D_MODEL),
        ys.reshape(bs, ns, D_MODEL),
        kt.reshape(depth, bp, KV_WIN, *hd),
        vt.reshape(depth, bp, KV_WIN, *hd),
        ks.reshape(depth, bs, ns, *hd),
        vs.reshape(depth, bs, ns, *hd),
        vas.reshape(depth, bs, ns, D_A),
    )
```

```python
import functools

import jax
import jax.numpy as jnp
from jax import lax
from jax.experimental import pallas as pl
from jax.experimental.pallas import tpu as pltpu

D_MODEL = 2048
D_A = 1024
D_B = 1024
GROUPS_A = 8
GROUP_A = 128
MLP_CHUNK = 128
N_HEADS = 8
HEAD_DIM = 128
CHUNK = 64
LEFT_CHUNKS = 8
KV_WIN = LEFT_CHUNKS * CHUNK
BAND = (LEFT_CHUNKS + 1) * CHUNK
GROUP_CHUNKS = 4
GROUP_Q = GROUP_CHUNKS * CHUNK
GROUP_K = GROUP_Q + KV_WIN
REL_CLIP = 128
D_FF = 5632
D_IN = 2 * D_A + 3 * D_B
EPS = 1e-6
NEG_INF = -1e30
SCALE = HEAD_DIM ** -0.5
LOG2E = 1.4426950408889634

F32 = jnp.float32
BF16 = jnp.bfloat16
LANES = 128

VMEM_LIMIT_BYTES = 60 * 1024 * 1024
FFN_ROW_CHUNK = 256
PROJ_ROW_CHUNK = 256
PROJ_ROW_TILE = 512
MIXER_ROW_TILE = 512
FFN_ROW_TILE = 1024
FFN_HIDDEN_TILE = 512
FFN_HEAD_HIDDEN_TILE = 256
CAST_SLABS = 64


def _rms(x, g):
    return x * lax.rsqrt(jnp.mean(x * x, axis=-1, keepdims=True) + EPS) * g


def _params(sem):
    return pltpu.CompilerParams(dimension_semantics=sem, vmem_limit_bytes=VMEM_LIMIT_BYTES)


def _ffn_rows(x_ref, o_ref, xn_ref, gpre_ref, gpost_ref, wg_ref, wu_ref, wd_ref, *, rc, first, last):
    for r in range(x_ref.shape[0] // rc):
        rows = slice(r * rc, (r + 1) * rc)
        if first:
            xn_ref[rows, :] = _rms(x_ref[rows, :], gpre_ref[...]).astype(BF16)
        xn = xn_ref[rows, :]
        parts = []
        for t in range(wg_ref.shape[0]):
            g = jnp.dot(xn, wg_ref[t], preferred_element_type=F32)
            u = jnp.dot(xn, wu_ref[t], preferred_element_type=F32)
            parts.append((g * jax.nn.sigmoid(g) * u).astype(wd_ref.dtype))
        a = parts[0] if len(parts) == 1 else jnp.concatenate(parts, axis=1)
        d = jnp.dot(a, wd_ref[...], preferred_element_type=F32)
        if first:
            o_ref[rows, :] = d
        elif last:
            o_ref[rows, :] = x_ref[rows, :] + 0.5 * _rms(o_ref[rows, :] + d, gpost_ref[...])
        else:
            o_ref[rows, :] += d


def _when_first_mid_last(j, nj, enabled, body):
    pl.when(jnp.logical_and(enabled, j == 0))(functools.partial(body, first=True, last=False))
    pl.when(jnp.logical_and(enabled, jnp.logical_and(j > 0, j < nj - 1)))(
        functools.partial(body, first=False, last=False))
    pl.when(jnp.logical_and(enabled, j == nj - 1))(functools.partial(body, first=False, last=True))


def _ffn_head_kernel(xp_ref, xs_ref, gpre_ref, gpost_ref, wgf_ref, wuf_ref, wd_ref,
                     op_ref, os_ref, wg_ref, wu_ref, xnp_ref, xns_ref, *, rc):
    def body(first, last):
        wg_ref[0] = wgf_ref[...].astype(BF16)
        wu_ref[0] = wuf_ref[...].astype(BF16)
        for x_ref, o_ref, xn_ref in ((xp_ref, op_ref, xnp_ref), (xs_ref, os_ref, xns_ref)):
            _ffn_rows(x_ref, o_ref, xn_ref, gpre_ref, gpost_ref, wg_ref, wu_ref, wd_ref,
                      rc=min(rc, x_ref.shape[0]), first=first, last=last)

    _when_first_mid_last(pl.program_id(0), pl.num_programs(0), True, body)


def _ffn_head(xp, xs, gpre, gpost, wg, wu, wd, *, tm, tf):
    nj = D_FF // tf
    ms = xs.shape[0]
    assert nj >= 2
    const = lambda j: (0, 0)
    vec = pl.BlockSpec((1, D_MODEL), const)
    return pl.pallas_call(
        functools.partial(_ffn_head_kernel, rc=FFN_ROW_CHUNK),
        grid=(nj,),
        in_specs=[
            pl.BlockSpec((tm, D_MODEL), const, pipeline_mode=pl.Buffered(1)),
            pl.BlockSpec((ms, D_MODEL), const, pipeline_mode=pl.Buffered(1)),
            vec, vec,
            pl.BlockSpec((D_MODEL, tf), lambda j: (0, j)),
            pl.BlockSpec((D_MODEL, tf), lambda j: (0, j)),
            pl.BlockSpec((tf, D_MODEL), lambda j: (j, 0)),
        ],
        out_specs=[
            pl.BlockSpec((tm, D_MODEL), const),
            pl.BlockSpec((ms, D_MODEL), const),
            pl.BlockSpec((1, D_MODEL, tf), lambda j: (j, 0, 0)),
            pl.BlockSpec((1, D_MODEL, tf), lambda j: (j, 0, 0)),
        ],
        out_shape=[
            jax.ShapeDtypeStruct((tm, D_MODEL), F32),
            jax.ShapeDtypeStruct((ms, D_MODEL), F32),
            jax.ShapeDtypeStruct((nj, D_MODEL, tf), BF16),
            jax.ShapeDtypeStruct((nj, D_MODEL, tf), BF16),
        ],
        scratch_shapes=[pltpu.VMEM((tm, D_MODEL), BF16), pltpu.VMEM((ms, D_MODEL), BF16)],
        compiler_params=_params(("arbitrary",)),
        name="ffn_head",
    )(xp, xs, gpre, gpost, wg, wu, wd)


def _ffn_kernel(x_ref, gpre_ref, gpost_ref, wg_ref, wu_ref, wd_ref, head_ref, *rest, rc, n_cast):
    cast_src, o_ref, cast_dst, xn_ref = rest[:n_cast], rest[n_cast], rest[n_cast + 1:-1], rest[-1]
    i = pl.program_id(0)
    j = pl.program_id(1)

    def body(first, last):
        for src, dst in zip(cast_src, cast_dst, strict=True):
            dst[...] = src[...].astype(BF16)
        _ffn_rows(x_ref, o_ref, xn_ref, gpre_ref, gpost_ref, wg_ref, wu_ref, wd_ref,
                  rc=rc, first=first, last=last)

    _when_first_mid_last(j, pl.num_programs(1), i > 0, body)

    @pl.when(jnp.logical_and(i == 0, j == 0))
    def _():
        pltpu.sync_copy(head_ref, o_ref)


def _ffn(x, gpre, gpost, wg, wu, wd, head, *, tm, tf, cast=()):
    m = x.shape[0]
    sub = tf // wg.shape[2]
    nj = wg.shape[0] // sub
    assert head.shape == (tm, D_MODEL) and m // tm >= 2 and nj >= 2
    assert (m // tm - 1) * nj >= CAST_SLABS
    hid = lambda i, j: jnp.where(i == 0, 0, j)
    slab = lambda i, j: (jnp.clip((i - 1) * nj + j, 0, CAST_SLABS - 1), 0)
    slab_spec = lambda w: pl.BlockSpec((w.shape[0] // CAST_SLABS, w.shape[1]), slab)
    vec = pl.BlockSpec((1, D_MODEL), lambda i, j: (0, 0))
    outs = pl.pallas_call(
        functools.partial(_ffn_kernel, rc=FFN_ROW_CHUNK, n_cast=len(cast)),
        grid=(m // tm, nj),
        in_specs=[
            pl.BlockSpec((tm, D_MODEL), lambda i, j: (jnp.maximum(i, 1), 0)), vec, vec,
            pl.BlockSpec((sub, D_MODEL, tf // sub), lambda i, j: (hid(i, j), 0, 0)),
            pl.BlockSpec((sub, D_MODEL, tf // sub), lambda i, j: (hid(i, j), 0, 0)),
            pl.BlockSpec((tf, D_MODEL), lambda i, j: (hid(i, j), 0)),
            pl.BlockSpec(memory_space=pl.ANY),
        ] + [slab_spec(w) for w in cast],
        out_specs=[pl.BlockSpec((tm, D_MODEL), lambda i, j: (i, 0))] + [slab_spec(w) for w in cast],
        out_shape=[jax.ShapeDtypeStruct((m, D_MODEL), F32)]
        + [jax.ShapeDtypeStruct(w.shape, BF16) for w in cast],
        scratch_shapes=[pltpu.VMEM((tm, D_MODEL), BF16)],
        compiler_params=_params(("arbitrary", "arbitrary")),
        name="ffn",
    )(x, gpre, gpost, wg, wu, wd, head, *cast)
    return outs if cast else outs[0]


def _inproj_rows(h_ref, hn_ref, gpre_ref, w_ref, gva_ref, u_ref, va_ref, q_ref, k_ref, v_ref, tails, rc):
    tm = h_ref.shape[0]
    for r in range(tm // rc):
        rows = slice(r * rc, (r + 1) * rc)
        hn_ref[rows, :] = _rms(h_ref[rows, :], gpre_ref[...]).astype(BF16)
        hn = hn_ref[rows, :]
        z = lambda c: jnp.dot(hn, w_ref[:, c * D_A:(c + 1) * D_A], preferred_element_type=F32)
        u_ref[rows, :] = jax.nn.gelu(z(0))
        va_ref[rows, :] = _rms(jax.nn.gelu(z(1)), gva_ref[...]).astype(va_ref.dtype)
        q_ref[rows, :] = z(2).astype(q_ref.dtype)
        for n, (c, o_ref) in enumerate(((3, k_ref), (4, v_ref))):
            zc = z(c)
            o_ref[rows, :] = zc.astype(o_ref.dtype)
            t0 = r * rc - (tm - KV_WIN)
            if tails is not None and t0 >= 0:
                tails[n][t0:t0 + rc, :] = zc


def _inproj_kernel(hp_ref, hs_ref, gpre_ref, w_ref, gva_ref,
                   up_ref, vap_ref, qp_ref, kp_ref, vp_ref, kt_ref, vt_ref,
                   us_ref, vas_ref, qs_ref, ks_ref, vs_ref, hnp_ref, hns_ref, *, rc):
    _inproj_rows(hp_ref, hnp_ref, gpre_ref, w_ref, gva_ref, up_ref, vap_ref, qp_ref, kp_ref, vp_ref,
                 (kt_ref, vt_ref), rc)

    @pl.when(pl.program_id(0) == 0)
    def _():
        _inproj_rows(hs_ref, hns_ref, gpre_ref, w_ref, gva_ref, us_ref, vas_ref, qs_ref, ks_ref, vs_ref,
                     None, min(rc, hs_ref.shape[0]))


def _inproj(hp, hs, gpre, w_in, gva, *, tm):
    m, ms = hp.shape[0], hs.shape[0]
    rc = min(tm, PROJ_ROW_CHUNK)
    assert tm >= KV_WIN and (tm - KV_WIN) % rc == 0
    row_blk = lambda i: (i, 0)
    const = lambda i: (0, 0)
    p_dtypes = (F32, BF16, BF16, BF16, BF16)
    s_dtypes = (F32, F32, BF16, F32, F32)
    return pl.pallas_call(
        functools.partial(_inproj_kernel, rc=rc),
        grid=(m // tm,),
        in_specs=[
            pl.BlockSpec((tm, D_MODEL), row_blk),
            pl.BlockSpec((ms, D_MODEL), const),
            pl.BlockSpec((1, D_MODEL), const),
            pl.BlockSpec((D_MODEL, D_IN), const, pipeline_mode=pl.Buffered(1)),
            pl.BlockSpec((1, D_A), const),
        ],
        out_specs=[pl.BlockSpec((tm, D_A), row_blk)] * 5 + [pl.BlockSpec((KV_WIN, D_B), const)] * 2
        + [pl.BlockSpec((ms, D_A), const)] * 5,
        out_shape=[jax.ShapeDtypeStruct((m, D_A), d) for d in p_dtypes]
        + [jax.ShapeDtypeStruct((KV_WIN, D_B), F32)] * 2
        + [jax.ShapeDtypeStruct((ms, D_A), d) for d in s_dtypes],
        scratch_shapes=[pltpu.VMEM((tm, D_MODEL), BF16), pltpu.VMEM((ms, D_MODEL), BF16)],
        compiler_params=_params(("arbitrary",)),
        name="inproj",
    )(hp, hs, gpre, w_in, gva)


def _mask_spatial_weights(sw_ref, wm_ref):
    t = lax.broadcasted_iota(jnp.int32, (MLP_CHUNK, MLP_CHUNK), 0)
    s = lax.broadcasted_iota(jnp.int32, (MLP_CHUNK, MLP_CHUNK), 1)
    for g in range(GROUPS_A):
        wm_ref[g] = jnp.where(s <= t, sw_ref[g], 0.0).astype(BF16)


def _attend(qh, kh, vh, bias, valid):
    s = lax.dot_general(qh, kh, (((1,), (1,)), ((), ())), preferred_element_type=F32) * SCALE + bias
    if valid is not None:
        s = jnp.where(valid, s, NEG_INF)
    e = jnp.exp(s - jnp.max(s, axis=-1, keepdims=True))
    p = e / jnp.sum(e, axis=-1, keepdims=True)
    return jnp.dot(p.astype(BF16), vh, preferred_element_type=F32)


def _gmlp_block(u_ref, va_ref, wm_ref, sbb_ref, goa_ref, cat_ref, a_scr):
    nchunk = u_ref.shape[0] // MLP_CHUNK
    chunk_rows = [slice(n * MLP_CHUNK, (n + 1) * MLP_CHUNK) for n in range(nchunk)]
    for g in range(GROUPS_A):
        cols = slice(g * GROUP_A, (g + 1) * GROUP_A)
        v_all = jnp.concatenate([va_ref[rows, cols] for rows in chunk_rows], axis=1)
        mixed = jnp.dot(wm_ref[g], v_all, preferred_element_type=F32)
        for n, rows in enumerate(chunk_rows):
            a_scr[rows, cols] = u_ref[rows, cols] * (mixed[:, n * GROUP_A:(n + 1) * GROUP_A] + sbb_ref[g])
    for rows in chunk_rows:
        cat_ref[rows, 0:D_A] = _rms(a_scr[rows, :], goa_ref[...]).astype(BF16)


def _attn_groups(q_ref, kbuf, vbuf, bias_ref, gob_ref, cat_ref, o_scr, *, first):
    tb = q_ref.shape[0]
    for gq in range(tb // GROUP_Q):
        q0 = gq * GROUP_Q
        qrows = slice(q0, q0 + GROUP_Q)
        krows = slice(q0, q0 + GROUP_K)
        if first:
            valid = lax.broadcasted_iota(jnp.int32, (GROUP_Q, GROUP_K), 1) >= KV_WIN - q0
        for h in range(N_HEADS):
            cols = slice(h * HEAD_DIM, (h + 1) * HEAD_DIM)
            s2 = lax.dot_general(q_ref[qrows, cols], kbuf[krows, cols], (((1,), (1,)), ((), ())),
                                 preferred_element_type=F32) * (SCALE * LOG2E) + bias_ref[h]
            if first:
                s2 = jnp.where(valid, s2, NEG_INF)
            e = jnp.exp2(s2 - jnp.max(s2, axis=-1, keepdims=True)).astype(BF16)
            nd = jnp.dot(e, vbuf[krows, 2 * h * HEAD_DIM:2 * (h + 1) * HEAD_DIM], preferred_element_type=F32)
            o_scr[:, cols] = nd[:, 0:HEAD_DIM] * (1.0 / nd[:, HEAD_DIM:2 * HEAD_DIM])
        cat_ref[qrows, D_A:D_A + D_B] = _rms(o_scr[...], gob_ref[...]).astype(BF16)


def _expand_rel_bias(relw_row, lq, lk):
    b = jnp.broadcast_to(relw_row, (lq, relw_row.shape[1]))
    return pltpu.roll(b, 0, 1, stride=1, stride_axis=0)[:, 0:lk]


def _mixer_prompt_kernel(u_ref, va_ref, q_ref, k_ref, v_ref, sw_ref, sbb_ref, relw_ref, goa_ref, gob_ref,
                         cat_ref, wm_ref, kbuf, vbuf, a_scr, o_scr, bias_ref):
    i = pl.program_id(0)
    tb = u_ref.shape[0]

    @pl.when(i == 0)
    def _():
        _mask_spatial_weights(sw_ref, wm_ref)
        qc = lax.broadcasted_iota(jnp.int32, (GROUP_Q, GROUP_K), 0) // CHUNK
        jc = lax.broadcasted_iota(jnp.int32, (GROUP_Q, GROUP_K), 1) // CHUNK
        band = jnp.logical_and(jc >= qc, jc <= qc + LEFT_CHUNKS)
        for h in range(N_HEADS):
            bias_ref[h] = jnp.where(band, _expand_rel_bias(relw_ref[h:h + 1, :], GROUP_Q, GROUP_K), NEG_INF)
        kbuf[0:KV_WIN, :] = jnp.zeros((KV_WIN, D_B), BF16)
        for h in range(N_HEADS):
            lo = 2 * h * HEAD_DIM
            vbuf[0:KV_WIN, lo:lo + HEAD_DIM] = jnp.zeros((KV_WIN, HEAD_DIM), BF16)
            vbuf[:, lo + HEAD_DIM:lo + 2 * HEAD_DIM] = jnp.ones((KV_WIN + tb, HEAD_DIM), BF16)

    @pl.when(i > 0)
    def _():
        kbuf[0:KV_WIN, :] = kbuf[tb:tb + KV_WIN, :]
        for h in range(N_HEADS):
            lo = 2 * h * HEAD_DIM
            vbuf[0:KV_WIN, lo:lo + HEAD_DIM] = vbuf[tb:tb + KV_WIN, lo:lo + HEAD_DIM]

    kbuf[KV_WIN:KV_WIN + tb, :] = k_ref[...]
    for h in range(N_HEADS):
        vbuf[KV_WIN:KV_WIN + tb, 2 * h * HEAD_DIM:(2 * h + 1) * HEAD_DIM] = v_ref[:, h * HEAD_DIM:(h + 1) * HEAD_DIM]

    _gmlp_block(u_ref, va_ref, wm_ref, sbb_ref, goa_ref, cat_ref, a_scr)

    @pl.when(i == 0)
    def _():
        _attn_groups(q_ref, kbuf, vbuf, bias_ref, gob_ref, cat_ref, o_scr, first=True)

    @pl.when(i > 0)
    def _():
        _attn_groups(q_ref, kbuf, vbuf, bias_ref, gob_ref, cat_ref, o_scr, first=False)


def _mixer_prompt(u, va, q, k, v, sw, sbb, relw, goa, gob, *, tb):
    m = u.shape[0]
    assert tb >= KV_WIN and tb % GROUP_Q == 0
    row_blk = lambda i: (i, 0)
    const2 = lambda i: (0, 0)
    const3 = lambda i: (0, 0, 0)
    return pl.pallas_call(
        _mixer_prompt_kernel,
        grid=(m // tb,),
        in_specs=[
            pl.BlockSpec((tb, D_A), row_blk),
            pl.BlockSpec((tb, D_A), row_blk),
            pl.BlockSpec((tb, D_B), row_blk),
            pl.BlockSpec((tb, D_B), row_blk),
            pl.BlockSpec((tb, D_B), row_blk),
            pl.BlockSpec((GROUPS_A, MLP_CHUNK, MLP_CHUNK), const3),
            pl.BlockSpec((GROUPS_A, MLP_CHUNK, GROUP_A), const3),
            pl.BlockSpec(relw.shape, const2),
            pl.BlockSpec((1, D_A), const2),
            pl.BlockSpec((1, D_B), const2),
        ],
        out_specs=pl.BlockSpec((tb, D_MODEL), row_blk),
        out_shape=jax.ShapeDtypeStruct((m, D_MODEL), BF16),
        scratch_shapes=[
            pltpu.VMEM((GROUPS_A, MLP_CHUNK, MLP_CHUNK), BF16),
            pltpu.VMEM((KV_WIN + tb, D_B), BF16),
            pltpu.VMEM((KV_WIN + tb, 2 * D_B), BF16),
            pltpu.VMEM((tb, D_A), F32),
            pltpu.VMEM((GROUP_Q, D_B), F32),
            pltpu.VMEM((N_HEADS, GROUP_Q, GROUP_K), F32),
        ],
        compiler_params=_params(("arbitrary",)),
        name="mixer_prompt",
    )(u, va, q, k, v, sw, sbb, relw, goa, gob)


def _mixer_sample_kernel(u_ref, va_ref, q_ref, k_ref, v_ref, ck_ref, cv_ref, sw_ref, sbb_ref, relw_ref,
                         goa_ref, gob_ref, cat_ref, wm_ref, vap, kbuf, vbuf, a_scr, o_scr, bias_ref):
    b = pl.program_id(0)
    n = u_ref.shape[0]
    w = ck_ref.shape[1] // N_HEADS

    @pl.when(b == 0)
    def _():
        _mask_spatial_weights(sw_ref, wm_ref)
        for h in range(N_HEADS):
            bias_ref[h] = _expand_rel_bias(relw_ref[h:h + 1, :], n, w + n)
        vap[...] = jnp.zeros(vap.shape, BF16)

    vap[0:n, :] = va_ref[...].astype(BF16)
    for h in range(N_HEADS):
        cols = slice(h * HEAD_DIM, (h + 1) * HEAD_DIM)
        kbuf[0:w, cols] = ck_ref[0, pl.ds(h, w, stride=N_HEADS), :].astype(BF16)
        vbuf[0:w, cols] = cv_ref[0, pl.ds(h, w, stride=N_HEADS), :].astype(BF16)
    kbuf[w:w + n, :] = k_ref[...].astype(BF16)
    vbuf[w:w + n, :] = v_ref[...].astype(BF16)

    ssq = jnp.zeros((n, 1), F32)
    for g in range(GROUPS_A):
        cols = slice(g * GROUP_A, (g + 1) * GROUP_A)
        mixed = jnp.dot(wm_ref[g, 0:n, :], vap[:, cols], preferred_element_type=F32) + sbb_ref[g, 0:n, :]
        a = u_ref[:, cols] * mixed
        a_scr[:, cols] = a
        ssq = ssq + jnp.sum(a * a, axis=-1, keepdims=True)
    inv = lax.rsqrt(ssq * (1.0 / D_A) + EPS)
    cat_ref[:, 0:D_A] = (a_scr[...] * inv * goa_ref[...]).astype(BF16)

    ssq = jnp.zeros((n, 1), F32)
    for h in range(N_HEADS):
        cols = slice(h * HEAD_DIM, (h + 1) * HEAD_DIM)
        o = _attend(q_ref[:, cols], kbuf[:, cols], vbuf[:, cols], bias_ref[h], None)
        o_scr[:, cols] = o
        ssq = ssq + jnp.sum(o * o, axis=-1, keepdims=True)
    inv = lax.rsqrt(ssq * (1.0 / D_B) + EPS)
    cat_ref[:, D_A:D_A + D_B] = (o_scr[...] * inv * gob_ref[...]).astype(BF16)


def _mixer_sample(u, va, q, k, v, ck, cv, sw, sbb, relw, goa, gob, *, n):
    m = u.shape[0]
    nb, wh, _ = ck.shape
    w = wh // N_HEADS
    row_blk = lambda b: (b, 0)
    const2 = lambda b: (0, 0)
    const3 = lambda b: (0, 0, 0)
    return pl.pallas_call(
        _mixer_sample_kernel,
        grid=(nb,),
        in_specs=[
            pl.BlockSpec((n, D_A), row_blk),
            pl.BlockSpec((n, D_A), row_blk),
            pl.BlockSpec((n, D_B), row_blk),
            pl.BlockSpec((n, D_B), row_blk),
            pl.BlockSpec((n, D_B), row_blk),
            pl.BlockSpec((1, wh, HEAD_DIM), lambda b: (b, 0, 0)),
            pl.BlockSpec((1, wh, HEAD_DIM), lambda b: (b, 0, 0)),
            pl.BlockSpec((GROUPS_A, MLP_CHUNK, MLP_CHUNK), const3),
            pl.BlockSpec((GROUPS_A, MLP_CHUNK, GROUP_A), const3),
            pl.BlockSpec(relw.shape, const2),
            pl.BlockSpec((1, D_A), const2),
            pl.BlockSpec((1, D_B), const2),
        ],
        out_specs=pl.BlockSpec((n, D_MODEL), row_blk),
        out_shape=jax.ShapeDtypeStruct((m, D_MODEL), BF16),
        scratch_shapes=[
            pltpu.VMEM((GROUPS_A, MLP_CHUNK, MLP_CHUNK), BF16),
            pltpu.VMEM((MLP_CHUNK, D_A), BF16),
            pltpu.VMEM((w + n, D_B), BF16),
            pltpu.VMEM((w + n, D_B), BF16),
            pltpu.VMEM((n, D_A), F32),
            pltpu.VMEM((n, D_B), F32),
            pltpu.VMEM((N_HEADS, n, w + n), F32),
        ],
        compiler_params=_params(("arbitrary",)),
        name="mixer_sample",
    )(u, va, q, k, v, ck, cv, sw, sbb, relw, goa, gob)


def _outproj_rows(cat_ref, h_ref, w_ref, g_ref, o_ref, rc):
    for r in range(cat_ref.shape[0] // rc):
        rows = slice(r * rc, (r + 1) * rc)
        m = jnp.dot(cat_ref[rows, :], w_ref[...], preferred_element_type=F32)
        o_ref[rows, :] = h_ref[rows, :] + _rms(m, g_ref[...])


def _outproj_kernel(catp_ref, hp_ref, cats_ref, hs_ref, w_ref, g_ref, op_ref, os_ref, *, rc):
    _outproj_rows(catp_ref, hp_ref, w_ref, g_ref, op_ref, rc)

    @pl.when(pl.program_id(0) == 0)
    def _():
        _outproj_rows(cats_ref, hs_ref, w_ref, g_ref, os_ref, min(rc, cats_ref.shape[0]))


def _outproj(catp, hp, cats, hs, w_out, g, *, tm):
    m, ms = catp.shape[0], cats.shape[0]
    row_blk = lambda i: (i, 0)
    const = lambda i: (0, 0)
    return pl.pallas_call(
        functools.partial(_outproj_kernel, rc=min(tm, PROJ_ROW_CHUNK)),
        grid=(m // tm,),
        in_specs=[
            pl.BlockSpec((tm, D_MODEL), row_blk),
            pl.BlockSpec((tm, D_MODEL), row_blk),
            pl.BlockSpec((ms, D_MODEL), const),
            pl.BlockSpec((ms, D_MODEL), const),
            pl.BlockSpec((D_MODEL, D_MODEL), const, pipeline_mode=pl.Buffered(1)),
            pl.BlockSpec((1, D_MODEL), const),
        ],
        out_specs=[pl.BlockSpec((tm, D_MODEL), row_blk), pl.BlockSpec((ms, D_MODEL), const)],
        out_shape=[jax.ShapeDtypeStruct((m, D_MODEL), F32), jax.ShapeDtypeStruct((ms, D_MODEL), F32)],
        compiler_params=_params(("arbitrary",)),
        name="outproj",
    )(catp, hp, cats, hs, w_out, g)


def _rel_bias_rows(rel_bias, lq, lk):
    nh = rel_bias.shape[0]
    p = lq + lk - 1
    t_min, t_max = KV_WIN - lk + 1, KV_WIN + lq - 1
    n_lo = max(0, -REL_CLIP - t_min)
    n_hi = max(0, t_max - REL_CLIP)
    mid = rel_bias[:, max(t_min, -REL_CLIP) + REL_CLIP:min(t_max, REL_CLIP) + REL_CLIP + 1]
    by_dist = jnp.concatenate([jnp.broadcast_to(rel_bias[:, :1], (nh, n_lo)), mid,
                               jnp.broadcast_to(rel_bias[:, -1:], (nh, n_hi))], axis=1).astype(F32)
    assert by_dist.shape == (nh, p)
    padded = jnp.pad(by_dist[:, ::-1], ((0, 0), (0, -p % LANES)))
    return jnp.roll(padded, -(lq - 1), axis=1)


def kernel(x_prompt, x_sample, cache_b_k, cache_b_v, w_in, w_out, sw_a, sb_a, rel_bias, g_va, g_oa, g_ob,
           g_f1_pre, g_f1_post, g_m_pre, g_m_post, g_f2_pre, g_f2_post,
           f1_gate, f1_up, f1_down, f2_gate, f2_up, f2_down):
    depth = w_in.shape[0]
    assert depth == 1
    bp, sp, _ = x_prompt.shape
    bs, ns, _ = x_sample.shape
    assert bp == 1
    l = 0
    row = lambda g: g[l].reshape(1, -1)
    sbb = jnp.broadcast_to(sb_a[l][:, :, None], (GROUPS_A, MLP_CHUNK, GROUP_A))
    bias_p = _rel_bias_rows(rel_bias[l], GROUP_Q, GROUP_K) * LOG2E
    wc = cache_b_k.shape[2]
    bias_s = _rel_bias_rows(rel_bias[l], ns, wc + ns)

    xp = x_prompt.reshape(sp, D_MODEL)
    ms = bs * ns
    xs = x_sample.reshape(ms, D_MODEL)
    ffn_tiles = dict(tm=FFN_ROW_TILE, tf=FFN_HIDDEN_TILE)

    h_head, hs, *f1 = _ffn_head(xp, xs, row(g_f1_pre), row(g_f1_post), f1_gate[l], f1_up[l], f1_down[l],
                                tm=FFN_ROW_TILE, tf=FFN_HEAD_HIDDEN_TILE)
    h, w_in_b, w_out_b = _ffn(xp, row(g_f1_pre), row(g_f1_post), *f1, f1_down[l], h_head, **ffn_tiles,
                              cast=(w_in[l], w_out[l]))

    u, va, q, k, v, kt, vt, us, vas, qs, ks, vs = _inproj(h, hs, row(g_m_pre), w_in_b, row(g_va),
                                                           tm=PROJ_ROW_TILE)
    cat = _mixer_prompt(u, va, q, k, v, sw_a[l], sbb, bias_p, row(g_oa), row(g_ob), tb=MIXER_ROW_TILE)
    ck = cache_b_k[l].reshape(bs, wc * N_HEADS, HEAD_DIM)
    cv = cache_b_v[l].reshape(bs, wc * N_HEADS, HEAD_DIM)
    cats = _mixer_sample(us, vas, qs, ks, vs, ck, cv, sw_a[l], sbb, bias_s, row(g_oa), row(g_ob), n=ns)
    h2, hs2 = _outproj(cat, h, cats, hs, w_out_b, row(g_m_post), tm=PROJ_ROW_TILE)

    y_head, ys, *f2 = _ffn_head(h2, hs2, row(g_f2_pre), row(g_f2_post), f2_gate[l], f2_up[l], f2_down[l],
                                tm=FFN_ROW_TILE, tf=FFN_HEAD_HIDDEN_TILE)
    yp = _ffn(h2, row(g_f2_pre), row(g_f2_post), *f2, f2_down[l], y_head, **ffn_tiles)

    hd = (N_HEADS, HEAD_DIM)
    return (
        yp.reshape(bp, sp, D_MODEL),
        ys.reshape(bs, ns, D_MODEL),
        kt.reshape(depth, bp, KV_WIN, *hd),
        vt.reshape(depth, bp, KV_WIN, *hd),
        ks.reshape(depth, bs, ns, *hd),
        vs.reshape(depth, bs, ns, *hd),
        vas.reshape(depth, bs, ns, D_A),
    )
```

```python
import functools

import jax
import jax.numpy as jnp
from jax import lax
from jax.experimental import pallas as pl
from jax.experimental.pallas import tpu as pltpu

D_MODEL = 2048
D_A = 1024
D_B = 1024
GROUPS_A = 8
GROUP_A = 128
MLP_CHUNK = 128
N_HEADS = 8
HEAD_DIM = 128
CHUNK = 64
LEFT_CHUNKS = 8
KV_WIN = LEFT_CHUNKS * CHUNK
BAND = (LEFT_CHUNKS + 1) * CHUNK
GROUP_CHUNKS = 4
GROUP_Q = GROUP_CHUNKS * CHUNK
GROUP_K = GROUP_Q + KV_WIN
REL_CLIP = 128
D_FF = 5632
D_IN = 2 * D_A + 3 * D_B
EPS = 1e-6
NEG_INF = -1e30
SCALE = HEAD_DIM ** -0.5
LOG2E = 1.4426950408889634

F32 = jnp.float32
BF16 = jnp.bfloat16
LANES = 128

VMEM_LIMIT_BYTES = 60 * 1024 * 1024
FFN_ROW_CHUNK = 256
PROJ_ROW_CHUNK = 256
PROJ_ROW_TILE = 512
MIXER_ROW_TILE = 512
FFN_ROW_TILE = 1024
FFN_HIDDEN_TILE = 512
FFN_HEAD_HIDDEN_TILE = 512
FFN_GATE_COLS = 256
CAST_SLABS = 64


def _rms(x, g):
    return x * lax.rsqrt(jnp.mean(x * x, axis=-1, keepdims=True) + EPS) * g


def _params(sem):
    return pltpu.CompilerParams(dimension_semantics=sem, vmem_limit_bytes=VMEM_LIMIT_BYTES)


def _ffn_rows(x_ref, o_ref, xn_ref, gpre_ref, gpost_ref, wg_ref, wu_ref, wd_ref, *, rc, first, last):
    for r in range(x_ref.shape[0] // rc):
        rows = slice(r * rc, (r + 1) * rc)
        if first:
            xn_ref[rows, :] = _rms(x_ref[rows, :], gpre_ref[...]).astype(BF16)
        xn = xn_ref[rows, :]
        parts = []
        tw = wg_ref.shape[2]
        for t in range(wg_ref.shape[0]):
            for c in range(0, tw, min(tw, FFN_GATE_COLS)):
                cols = slice(c, c + min(tw, FFN_GATE_COLS))
                g = jnp.dot(xn, wg_ref[t, :, cols], preferred_element_type=F32)
                u = jnp.dot(xn, wu_ref[t, :, cols], preferred_element_type=F32)
                parts.append((g * jax.nn.sigmoid(g) * u).astype(wd_ref.dtype))
        a = parts[0] if len(parts) == 1 else jnp.concatenate(parts, axis=1)
        d = jnp.dot(a, wd_ref[...], preferred_element_type=F32)
        if first:
            o_ref[rows, :] = d
        elif last:
            o_ref[rows, :] = x_ref[rows, :] + 0.5 * _rms(o_ref[rows, :] + d, gpost_ref[...])
        else:
            o_ref[rows, :] += d


def _when_first_mid_last(j, nj, enabled, body):
    pl.when(jnp.logical_and(enabled, j == 0))(functools.partial(body, first=True, last=False))
    pl.when(jnp.logical_and(enabled, jnp.logical_and(j > 0, j < nj - 1)))(
        functools.partial(body, first=False, last=False))
    pl.when(jnp.logical_and(enabled, j == nj - 1))(functools.partial(body, first=False, last=True))


def _ffn_head_kernel(xp_ref, xs_ref, gpre_ref, gpost_ref, wgf_hbm, wuf_hbm, wd_ref,
                     op_ref, os_ref, wg_ref, wu_ref, xnp_ref, xns_ref, wgf_buf, wuf_buf, sems, *, rc):
    j = pl.program_id(0)
    nj = pl.num_programs(0)
    tf = wgf_buf.shape[2]
    slot = j % 2

    def tile_copies(tile, s):
        cols = pl.ds(pl.multiple_of(tile * tf, tf), tf)
        return [pltpu.make_async_copy(hbm.at[:, cols], buf.at[s], sems.at[m, s])
                for m, (hbm, buf) in enumerate(((wgf_hbm, wgf_buf), (wuf_hbm, wuf_buf)))]

    def start(tile, s):
        for priority, copy in enumerate(tile_copies(tile, s)):
            copy.start(priority=priority)

    @pl.when(j == 0)
    def _():
        start(0, 0)

    @pl.when(j + 1 < nj)
    def _():
        start(j + 1, 1 - slot)

    for copy in tile_copies(j, slot):
        copy.wait()

    def body(first, last):
        wg_ref[0] = wgf_buf[slot].astype(BF16)
        wu_ref[0] = wuf_buf[slot].astype(BF16)
        for x_ref, o_ref, xn_ref in ((xp_ref, op_ref, xnp_ref), (xs_ref, os_ref, xns_ref)):
            _ffn_rows(x_ref, o_ref, xn_ref, gpre_ref, gpost_ref, wg_ref, wu_ref, wd_ref,
                      rc=min(rc, x_ref.shape[0]), first=first, last=last)

    _when_first_mid_last(j, nj, True, body)


def _ffn_head(xp, xs, gpre, gpost, wg, wu, wd, *, tm, tf):
    nj = D_FF // tf
    ms = xs.shape[0]
    assert nj >= 2
    const = lambda j: (0, 0)
    vec = pl.BlockSpec((1, D_MODEL), const)
    return pl.pallas_call(
        functools.partial(_ffn_head_kernel, rc=FFN_ROW_CHUNK),
        grid=(nj,),
        in_specs=[
            pl.BlockSpec((tm, D_MODEL), const, pipeline_mode=pl.Buffered(1)),
            pl.BlockSpec((ms, D_MODEL), const, pipeline_mode=pl.Buffered(1)),
            vec, vec,
            pl.BlockSpec(memory_space=pl.ANY),
            pl.BlockSpec(memory_space=pl.ANY),
            pl.BlockSpec((tf, D_MODEL), lambda j: (j, 0)),
        ],
        out_specs=[
            pl.BlockSpec((tm, D_MODEL), const),
            pl.BlockSpec((ms, D_MODEL), const),
            pl.BlockSpec((1, D_MODEL, tf), lambda j: (j, 0, 0)),
            pl.BlockSpec((1, D_MODEL, tf), lambda j: (j, 0, 0)),
        ],
        out_shape=[
            jax.ShapeDtypeStruct((tm, D_MODEL), F32),
            jax.ShapeDtypeStruct((ms, D_MODEL), F32),
            jax.ShapeDtypeStruct((nj, D_MODEL, tf), BF16),
            jax.ShapeDtypeStruct((nj, D_MODEL, tf), BF16),
        ],
        scratch_shapes=[
            pltpu.VMEM((tm, D_MODEL), BF16), pltpu.VMEM((ms, D_MODEL), BF16),
            pltpu.VMEM((2, D_MODEL, tf), F32), pltpu.VMEM((2, D_MODEL, tf), F32),
            pltpu.SemaphoreType.DMA((2, 2)),
        ],
        compiler_params=_params(("arbitrary",)),
        name="ffn_head",
    )(xp, xs, gpre, gpost, wg, wu, wd)


def _ffn_kernel(x_ref, gpre_ref, gpost_ref, wg_ref, wu_ref, wd_ref, head_ref, *rest, rc, n_cast):
    cast_src, o_ref, cast_dst, xn_ref = rest[:n_cast], rest[n_cast], rest[n_cast + 1:-1], rest[-1]
    i = pl.program_id(0)
    j = pl.program_id(1)

    def body(first, last):
        for src, dst in zip(cast_src, cast_dst, strict=True):
            dst[...] = src[...].astype(BF16)
        _ffn_rows(x_ref, o_ref, xn_ref, gpre_ref, gpost_ref, wg_ref, wu_ref, wd_ref,
                  rc=rc, first=first, last=last)

    _when_first_mid_last(j, pl.num_programs(1), i > 0, body)

    @pl.when(jnp.logical_and(i == 0, j == 0))
    def _():
        pltpu.sync_copy(head_ref, o_ref)


def _ffn(x, gpre, gpost, wg, wu, wd, head, *, tm, tf, cast=()):
    m = x.shape[0]
    sub = tf // wg.shape[2]
    nj = wg.shape[0] // sub
    assert head.shape == (tm, D_MODEL) and m // tm >= 2 and nj >= 2
    assert (m // tm - 1) * nj >= CAST_SLABS
    hid = lambda i, j: jnp.where(i == 0, 0, j)
    slab = lambda i, j: (jnp.clip((i - 1) * nj + j, 0, CAST_SLABS - 1), 0)
    slab_spec = lambda w: pl.BlockSpec((w.shape[0] // CAST_SLABS, w.shape[1]), slab)
    vec = pl.BlockSpec((1, D_MODEL), lambda i, j: (0, 0))
    outs = pl.pallas_call(
        functools.partial(_ffn_kernel, rc=FFN_ROW_CHUNK, n_cast=len(cast)),
        grid=(m // tm, nj),
        in_specs=[
            pl.BlockSpec((tm, D_MODEL), lambda i, j: (jnp.maximum(i, 1), 0)), vec, vec,
            pl.BlockSpec((sub, D_MODEL, tf // sub), lambda i, j: (hid(i, j), 0, 0)),
            pl.BlockSpec((sub, D_MODEL, tf // sub), lambda i, j: (hid(i, j), 0, 0)),
            pl.BlockSpec((tf, D_MODEL), lambda i, j: (hid(i, j), 0)),
            pl.BlockSpec(memory_space=pl.ANY),
        ] + [slab_spec(w) for w in cast],
        out_specs=[pl.BlockSpec((tm, D_MODEL), lambda i, j: (i, 0))] + [slab_spec(w) for w in cast],
        out_shape=[jax.ShapeDtypeStruct((m, D_MODEL), F32)]
        + [jax.ShapeDtypeStruct(w.shape, BF16) for w in cast],
        scratch_shapes=[pltpu.VMEM((tm, D_MODEL), BF16)],
        compiler_params=_params(("arbitrary", "arbitrary")),
        name="ffn",
    )(x, gpre, gpost, wg, wu, wd, head, *cast)
    return outs if cast else outs[0]


def _inproj_rows(h_ref, hn_ref, gpre_ref, w_ref, gva_ref, u_ref, va_ref, q_ref, k_ref, v_ref, tails, rc):
    tm = h_ref.shape[0]
    for r in range(tm // rc):
        rows = slice(r * rc, (r + 1) * rc)
        hn_ref[rows, :] = _rms(h_ref[rows, :], gpre_ref[...]).astype(BF16)
        hn = hn_ref[rows, :]
        z = lambda c: jnp.dot(hn, w_ref[:, c * D_A:(c + 1) * D_A], preferred_element_type=F32)
        u_ref[rows, :] = jax.nn.gelu(z(0))
        va_ref[rows, :] = _rms(jax.nn.gelu(z(1)), gva_ref[...]).astype(va_ref.dtype)
        q_ref[rows, :] = z(2).astype(q_ref.dtype)
        for n, (c, o_ref) in enumerate(((3, k_ref), (4, v_ref))):
            zc = z(c)
            o_ref[rows, :] = zc.astype(o_ref.dtype)
            t0 = r * rc - (tm - KV_WIN)
            if tails is not None and t0 >= 0:
                tails[n][t0:t0 + rc, :] = zc


def _inproj_kernel(hp_ref, hs_ref, gpre_ref, w_ref, gva_ref,
                   up_ref, vap_ref, qp_ref, kp_ref, vp_ref, kt_ref, vt_ref,
                   us_ref, vas_ref, qs_ref, ks_ref, vs_ref, hnp_ref, hns_ref, *, rc):
    _inproj_rows(hp_ref, hnp_ref, gpre_ref, w_ref, gva_ref, up_ref, vap_ref, qp_ref, kp_ref, vp_ref,
                 (kt_ref, vt_ref), rc)

    @pl.when(pl.program_id(0) == 0)
    def _():
        _inproj_rows(hs_ref, hns_ref, gpre_ref, w_ref, gva_ref, us_ref, vas_ref, qs_ref, ks_ref, vs_ref,
                     None, min(rc, hs_ref.shape[0]))


def _inproj(hp, hs, gpre, w_in, gva, *, tm):
    m, ms = hp.shape[0], hs.shape[0]
    rc = min(tm, PROJ_ROW_CHUNK)
    assert tm >= KV_WIN and (tm - KV_WIN) % rc == 0
    row_blk = lambda i: (i, 0)
    const = lambda i: (0, 0)
    p_dtypes = (F32, BF16, BF16, BF16, BF16)
    s_dtypes = (F32, F32, BF16, F32, F32)
    return pl.pallas_call(
        functools.partial(_inproj_kernel, rc=rc),
        grid=(m // tm,),
        in_specs=[
            pl.BlockSpec((tm, D_MODEL), row_blk),
            pl.BlockSpec((ms, D_MODEL), const),
            pl.BlockSpec((1, D_MODEL), const),
            pl.BlockSpec((D_MODEL, D_IN), const, pipeline_mode=pl.Buffered(1)),
            pl.BlockSpec((1, D_A), const),
        ],
        out_specs=[pl.BlockSpec((tm, D_A), row_blk)] * 5 + [pl.BlockSpec((KV_WIN, D_B), const)] * 2
        + [pl.BlockSpec((ms, D_A), const)] * 5,
        out_shape=[jax.ShapeDtypeStruct((m, D_A), d) for d in p_dtypes]
        + [jax.ShapeDtypeStruct((KV_WIN, D_B), F32)] * 2
        + [jax.ShapeDtypeStruct((ms, D_A), d) for d in s_dtypes],
        scratch_shapes=[pltpu.VMEM((tm, D_MODEL), BF16), pltpu.VMEM((ms, D_MODEL), BF16)],
        compiler_params=_params(("arbitrary",)),
        name="inproj",
    )(hp, hs, gpre, w_in, gva)


def _mask_spatial_weights(sw_ref, wm_ref):
    t = lax.broadcasted_iota(jnp.int32, (MLP_CHUNK, MLP_CHUNK), 0)
    s = lax.broadcasted_iota(jnp.int32, (MLP_CHUNK, MLP_CHUNK), 1)
    for g in range(GROUPS_A):
        wm_ref[g] = jnp.where(s <= t, sw_ref[g], 0.0).astype(BF16)


def _attend(qh, kh, vh, bias, valid):
    s = lax.dot_general(qh, kh, (((1,), (1,)), ((), ())), preferred_element_type=F32) * SCALE + bias
    if valid is not None:
        s = jnp.where(valid, s, NEG_INF)
    e = jnp.exp(s - jnp.max(s, axis=-1, keepdims=True))
    p = e / jnp.sum(e, axis=-1, keepdims=True)
    return jnp.dot(p.astype(BF16), vh, preferred_element_type=F32)


def _gmlp_block(u_ref, va_ref, wm_ref, sbb_ref, goa_ref, cat_ref, a_scr):
    nchunk = u_ref.shape[0] // MLP_CHUNK
    chunk_rows = [slice(n * MLP_CHUNK, (n + 1) * MLP_CHUNK) for n in range(nchunk)]
    for g in range(GROUPS_A):
        cols = slice(g * GROUP_A, (g + 1) * GROUP_A)
        v_all = jnp.concatenate([va_ref[rows, cols] for rows in chunk_rows], axis=1)
        mixed = jnp.dot(wm_ref[g], v_all, preferred_element_type=F32)
        for n, rows in enumerate(chunk_rows):
            a_scr[rows, cols] = u_ref[rows, cols] * (mixed[:, n * GROUP_A:(n + 1) * GROUP_A] + sbb_ref[g])
    for rows in chunk_rows:
        cat_ref[rows, 0:D_A] = _rms(a_scr[rows, :], goa_ref[...]).astype(BF16)


def _attn_groups(q_ref, kbuf, vbuf, bias_ref, gob_ref, cat_ref, o_scr, *, first):
    tb = q_ref.shape[0]
    for gq in range(tb // GROUP_Q):
        q0 = gq * GROUP_Q
        qrows = slice(q0, q0 + GROUP_Q)
        krows = slice(q0, q0 + GROUP_K)
        if first:
            valid = lax.broadcasted_iota(jnp.int32, (GROUP_Q, GROUP_K), 1) >= KV_WIN - q0
        for h in range(N_HEADS):
            cols = slice(h * HEAD_DIM, (h + 1) * HEAD_DIM)
            s2 = lax.dot_general(q_ref[qrows, cols], kbuf[krows, cols], (((1,), (1,)), ((), ())),
                                 preferred_element_type=F32) * (SCALE * LOG2E) + bias_ref[h]
            if first:
                s2 = jnp.where(valid, s2, NEG_INF)
            e = jnp.exp2(s2 - jnp.max(s2, axis=-1, keepdims=True)).astype(BF16)
            nd = jnp.dot(e, vbuf[krows, 2 * h * HEAD_DIM:2 * (h + 1) * HEAD_DIM], preferred_element_type=F32)
            o_scr[:, cols] = nd[:, 0:HEAD_DIM] * (1.0 / nd[:, HEAD_DIM:2 * HEAD_DIM])
        cat_ref[qrows, D_A:D_A + D_B] = _rms(o_scr[...], gob_ref[...]).astype(BF16)


def _expand_rel_bias(relw_row, lq, lk):
    b = jnp.broadcast_to(relw_row, (lq, relw_row.shape[1]))
    return pltpu.roll(b, 0, 1, stride=1, stride_axis=0)[:, 0:lk]


def _mixer_prompt_kernel(u_ref, va_ref, q_ref, k_ref, v_ref, sw_ref, sbb_ref, relw_ref, goa_ref, gob_ref,
                         cat_ref, wm_ref, kbuf, vbuf, a_scr, o_scr, bias_ref):
    i = pl.program_id(0)
    tb = u_ref.shape[0]

    @pl.when(i == 0)
    def _():
        _mask_spatial_weights(sw_ref, wm_ref)
        qc = lax.broadcasted_iota(jnp.int32, (GROUP_Q, GROUP_K), 0) // CHUNK
        jc = lax.broadcasted_iota(jnp.int32, (GROUP_Q, GROUP_K), 1) // CHUNK
        band = jnp.logical_and(jc >= qc, jc <= qc + LEFT_CHUNKS)
        for h in range(N_HEADS):
            bias_ref[h] = jnp.where(band, _expand_rel_bias(relw_ref[h:h + 1, :], GROUP_Q, GROUP_K), NEG_INF)
        kbuf[0:KV_WIN, :] = jnp.zeros((KV_WIN, D_B), BF16)
        for h in range(N_HEADS):
            lo = 2 * h * HEAD_DIM
            vbuf[0:KV_WIN, lo:lo + HEAD_DIM] = jnp.zeros((KV_WIN, HEAD_DIM), BF16)
            vbuf[:, lo + HEAD_DIM:lo + 2 * HEAD_DIM] = jnp.ones((KV_WIN + tb, HEAD_DIM), BF16)

    @pl.when(i > 0)
    def _():
        kbuf[0:KV_WIN, :] = kbuf[tb:tb + KV_WIN, :]
        for h in range(N_HEADS):
            lo = 2 * h * HEAD_DIM
            vbuf[0:KV_WIN, lo:lo + HEAD_DIM] = vbuf[tb:tb + KV_WIN, lo:lo + HEAD_DIM]

    kbuf[KV_WIN:KV_WIN + tb, :] = k_ref[...]
    for h in range(N_HEADS):
        vbuf[KV_WIN:KV_WIN + tb, 2 * h * HEAD_DIM:(2 * h + 1) * HEAD_DIM] = v_ref[:, h * HEAD_DIM:(h + 1) * HEAD_DIM]

    _gmlp_block(u_ref, va_ref, wm_ref, sbb_ref, goa_ref, cat_ref, a_scr)

    @pl.when(i == 0)
    def _():
        _attn_groups(q_ref, kbuf, vbuf, bias_ref, gob_ref, cat_ref, o_scr, first=True)

    @pl.when(i > 0)
    def _():
        _attn_groups(q_ref, kbuf, vbuf, bias_ref, gob_ref, cat_ref, o_scr, first=False)


def _mixer_prompt(u, va, q, k, v, sw, sbb, relw, goa, gob, *, tb):
    m = u.shape[0]
    assert tb >= KV_WIN and tb % GROUP_Q == 0
    row_blk = lambda i: (i, 0)
    const2 = lambda i: (0, 0)
    const3 = lambda i: (0, 0, 0)
    return pl.pallas_call(
        _mixer_prompt_kernel,
        grid=(m // tb,),
        in_specs=[
            pl.BlockSpec((tb, D_A), row_blk),
            pl.BlockSpec((tb, D_A), row_blk),
            pl.BlockSpec((tb, D_B), row_blk),
            pl.BlockSpec((tb, D_B), row_blk),
            pl.BlockSpec((tb, D_B), row_blk),
            pl.BlockSpec((GROUPS_A, MLP_CHUNK, MLP_CHUNK), const3),
            pl.BlockSpec((GROUPS_A, MLP_CHUNK, GROUP_A), const3),
            pl.BlockSpec(relw.shape, const2),
            pl.BlockSpec((1, D_A), const2),
            pl.BlockSpec((1, D_B), const2),
        ],
        out_specs=pl.BlockSpec((tb, D_MODEL), row_blk),
        out_shape=jax.ShapeDtypeStruct((m, D_MODEL), BF16),
        scratch_shapes=[
            pltpu.VMEM((GROUPS_A, MLP_CHUNK, MLP_CHUNK), BF16),
            pltpu.VMEM((KV_WIN + tb, D_B), BF16),
            pltpu.VMEM((KV_WIN + tb, 2 * D_B), BF16),
            pltpu.VMEM((tb, D_A), F32),
            pltpu.VMEM((GROUP_Q, D_B), F32),
            pltpu.VMEM((N_HEADS, GROUP_Q, GROUP_K), F32),
        ],
        compiler_params=_params(("arbitrary",)),
        name="mixer_prompt",
    )(u, va, q, k, v, sw, sbb, relw, goa, gob)


def _mixer_sample_kernel(u_ref, va_ref, q_ref, k_ref, v_ref, ck_ref, cv_ref, sw_ref, sbb_ref, relw_ref,
                         goa_ref, gob_ref, cat_ref, wm_ref, vap, kbuf, vbuf, a_scr, o_scr, bias_ref):
    b = pl.program_id(0)
    n = u_ref.shape[0]
    w = ck_ref.shape[1] // N_HEADS

    @pl.when(b == 0)
    def _():
        _mask_spatial_weights(sw_ref, wm_ref)
        for h in range(N_HEADS):
            bias_ref[h] = _expand_rel_bias(relw_ref[h:h + 1, :], n, w + n)
        vap[...] = jnp.zeros(vap.shape, BF16)

    vap[0:n, :] = va_ref[...].astype(BF16)
    for h in range(N_HEADS):
        cols = slice(h * HEAD_DIM, (h + 1) * HEAD_DIM)
        kbuf[0:w, cols] = ck_ref[0, pl.ds(h, w, stride=N_HEADS), :].astype(BF16)
        vbuf[0:w, cols] = cv_ref[0, pl.ds(h, w, stride=N_HEADS), :].astype(BF16)
    kbuf[w:w + n, :] = k_ref[...].astype(BF16)
    vbuf[w:w + n, :] = v_ref[...].astype(BF16)

    ssq = jnp.zeros((n, 1), F32)
    for g in range(GROUPS_A):
        cols = slice(g * GROUP_A, (g + 1) * GROUP_A)
        mixed = jnp.dot(wm_ref[g, 0:n, :], vap[:, cols], preferred_element_type=F32) + sbb_ref[g, 0:n, :]
        a = u_ref[:, cols] * mixed
        a_scr[:, cols] = a
        ssq = ssq + jnp.sum(a * a, axis=-1, keepdims=True)
    inv = lax.rsqrt(ssq * (1.0 / D_A) + EPS)
    cat_ref[:, 0:D_A] = (a_scr[...] * inv * goa_ref[...]).astype(BF16)

    ssq = jnp.zeros((n, 1), F32)
    for h in range(N_HEADS):
        cols = slice(h * HEAD_DIM, (h + 1) * HEAD_DIM)
        o = _attend(q_ref[:, cols], kbuf[:, cols], vbuf[:, cols], bias_ref[h], None)
        o_scr[:, cols] = o
        ssq = ssq + jnp.sum(o * o, axis=-1, keepdims=True)
    inv = lax.rsqrt(ssq * (1.0 / D_B) + EPS)
    cat_ref[:, D_A:D_A + D_B] = (o_scr[...] * inv * gob_ref[...]).astype(BF16)


def _mixer_sample(u, va, q, k, v, ck, cv, sw, sbb, relw, goa, gob, *, n):
    m = u.shape[0]
    nb, wh, _ = ck.shape
    w = wh // N_HEADS
    row_blk = lambda b: (b, 0)
    const2 = lambda b: (0, 0)
    const3 = lambda b: (0, 0, 0)
    return pl.pallas_call(
        _mixer_sample_kernel,
        grid=(nb,),
        in_specs=[
            pl.BlockSpec((n, D_A), row_blk),
            pl.BlockSpec((n, D_A), row_blk),
            pl.BlockSpec((n, D_B), row_blk),
            pl.BlockSpec((n, D_B), row_blk),
            pl.BlockSpec((n, D_B), row_blk),
            pl.BlockSpec((1, wh, HEAD_DIM), lambda b: (b, 0, 0)),
            pl.BlockSpec((1, wh, HEAD_DIM), lambda b: (b, 0, 0)),
            pl.BlockSpec((GROUPS_A, MLP_CHUNK, MLP_CHUNK), const3),
            pl.BlockSpec((GROUPS_A, MLP_CHUNK, GROUP_A), const3),
            pl.BlockSpec(relw.shape, const2),
            pl.BlockSpec((1, D_A), const2),
            pl.BlockSpec((1, D_B), const2),
        ],
        out_specs=pl.BlockSpec((n, D_MODEL), row_blk),
        out_shape=jax.ShapeDtypeStruct((m, D_MODEL), BF16),
        scratch_shapes=[
            pltpu.VMEM((GROUPS_A, MLP_CHUNK, MLP_CHUNK), BF16),
            pltpu.VMEM((MLP_CHUNK, D_A), BF16),
            pltpu.VMEM((w + n, D_B), BF16),
            pltpu.VMEM((w + n, D_B), BF16),
            pltpu.VMEM((n, D_A), F32),
            pltpu.VMEM((n, D_B), F32),
            pltpu.VMEM((N_HEADS, n, w + n), F32),
        ],
        compiler_params=_params(("arbitrary",)),
        name="mixer_sample",
    )(u, va, q, k, v, ck, cv, sw, sbb, relw, goa, gob)


def _outproj_rows(cat_ref, h_ref, w_ref, g_ref, o_ref, rc):
    for r in range(cat_ref.shape[0] // rc):
        rows = slice(r * rc, (r + 1) * rc)
        m = jnp.dot(cat_ref[rows, :], w_ref[...], preferred_element_type=F32)
        o_ref[rows, :] = h_ref[rows, :] + _rms(m, g_ref[...])


def _outproj_kernel(catp_ref, hp_ref, cats_ref, hs_ref, w_ref, g_ref, op_ref, os_ref, *, rc):
    _outproj_rows(catp_ref, hp_ref, w_ref, g_ref, op_ref, rc)

    @pl.when(pl.program_id(0) == 0)
    def _():
        _outproj_rows(cats_ref, hs_ref, w_ref, g_ref, os_ref, min(rc, cats_ref.shape[0]))


def _outproj(catp, hp, cats, hs, w_out, g, *, tm):
    m, ms = catp.shape[0], cats.shape[0]
    row_blk = lambda i: (i, 0)
    const = lambda i: (0, 0)
    return pl.pallas_call(
        functools.partial(_outproj_kernel, rc=min(tm, PROJ_ROW_CHUNK)),
        grid=(m // tm,),
        in_specs=[
            pl.BlockSpec((tm, D_MODEL), row_blk),
            pl.BlockSpec((tm, D_MODEL), row_blk),
            pl.BlockSpec((ms, D_MODEL), const),
            pl.BlockSpec((ms, D_MODEL), const),
            pl.BlockSpec((D_MODEL, D_MODEL), const, pipeline_mode=pl.Buffered(1)),
            pl.BlockSpec((1, D_MODEL), const),
        ],
        out_specs=[pl.BlockSpec((tm, D_MODEL), row_blk), pl.BlockSpec((ms, D_MODEL), const)],
        out_shape=[jax.ShapeDtypeStruct((m, D_MODEL), F32), jax.ShapeDtypeStruct((ms, D_MODEL), F32)],
        compiler_params=_params(("arbitrary",)),
        name="outproj",
    )(catp, hp, cats, hs, w_out, g)


def _rel_bias_rows(rel_bias, lq, lk):
    nh = rel_bias.shape[0]
    p = lq + lk - 1
    t_min, t_max = KV_WIN - lk + 1, KV_WIN + lq - 1
    n_lo = max(0, -REL_CLIP - t_min)
    n_hi = max(0, t_max - REL_CLIP)
    mid = rel_bias[:, max(t_min, -REL_CLIP) + REL_CLIP:min(t_max, REL_CLIP) + REL_CLIP + 1]
    by_dist = jnp.concatenate([jnp.broadcast_to(rel_bias[:, :1], (nh, n_lo)), mid,
                               jnp.broadcast_to(rel_bias[:, -1:], (nh, n_hi))], axis=1).astype(F32)
    assert by_dist.shape == (nh, p)
    padded = jnp.pad(by_dist[:, ::-1], ((0, 0), (0, -p % LANES)))
    return jnp.roll(padded, -(lq - 1), axis=1)


def kernel(x_prompt, x_sample, cache_b_k, cache_b_v, w_in, w_out, sw_a, sb_a, rel_bias, g_va, g_oa, g_ob,
           g_f1_pre, g_f1_post, g_m_pre, g_m_post, g_f2_pre, g_f2_post,
           f1_gate, f1_up, f1_down, f2_gate, f2_up, f2_down):
    depth = w_in.shape[0]
    assert depth == 1
    bp, sp, _ = x_prompt.shape
    bs, ns, _ = x_sample.shape
    assert bp == 1
    l = 0
    row = lambda g: g[l].reshape(1, -1)
    sbb = jnp.broadcast_to(sb_a[l][:, :, None], (GROUPS_A, MLP_CHUNK, GROUP_A))
    bias_p = _rel_bias_rows(rel_bias[l], GROUP_Q, GROUP_K) * LOG2E
    wc = cache_b_k.shape[2]
    bias_s = _rel_bias_rows(rel_bias[l], ns, wc + ns)

    xp = x_prompt.reshape(sp, D_MODEL)
    ms = bs * ns
    xs = x_sample.reshape(ms, D_MODEL)
    ffn_tiles = dict(tm=FFN_ROW_TILE, tf=FFN_HIDDEN_TILE)

    h_head, hs, *f1 = _ffn_head(xp, xs, row(g_f1_pre), row(g_f1_post), f1_gate[l], f1_up[l], f1_down[l],
                                tm=FFN_ROW_TILE, tf=FFN_HEAD_HIDDEN_TILE)
    h, w_in_b, w_out_b = _ffn(xp, row(g_f1_pre), row(g_f1_post), *f1, f1_down[l], h_head, **ffn_tiles,
                              cast=(w_in[l], w_out[l]))

    u, va, q, k, v, kt, vt, us, vas, qs, ks, vs = _inproj(h, hs, row(g_m_pre), w_in_b, row(g_va),
                                                           tm=PROJ_ROW_TILE)
    cat = _mixer_prompt(u, va, q, k, v, sw_a[l], sbb, bias_p, row(g_oa), row(g_ob), tb=MIXER_ROW_TILE)
    ck = cache_b_k[l].reshape(bs, wc * N_HEADS, HEAD_DIM)
    cv = cache_b_v[l].reshape(bs, wc * N_HEADS, HEAD_DIM)
    cats = _mixer_sample(us, vas, qs, ks, vs, ck, cv, sw_a[l], sbb, bias_s, row(g_oa), row(g_ob), n=ns)
    h2, hs2 = _outproj(cat, h, cats, hs, w_out_b, row(g_m_post), tm=PROJ_ROW_TILE)

    y_head, ys, *f2 = _ffn_head(h2, hs2, row(g_f2_pre), row(g_f2_post), f2_gate[l], f2_up[l], f2_down[l],
                                tm=FFN_ROW_TILE, tf=FFN_HEAD_HIDDEN_TILE)
    yp = _ffn(h2, row(g_f2_pre), row(g_f2_post), *f2, f2_down[l], y_head, **ffn_tiles)

    hd = (N_HEADS, HEAD_DIM)
    return (
        yp.reshape(bp, sp, D_MODEL),
        ys.reshape(bs, ns, D_MODEL),
        kt.reshape(depth, bp, KV_WIN, *hd),
        vt.reshape(depth, bp, KV_WIN, *hd),
        ks.reshape(depth, bs, ns, *hd),
        vs.reshape(depth, bs, ns, *hd),
        vas.reshape(depth, bs, ns, D_A),
    )
```

```python
import functools

import jax
import jax.numpy as jnp
from jax import lax
from jax.experimental import pallas as pl
from jax.experimental.pallas import tpu as pltpu

D_MODEL = 2048
D_A = 1024
D_B = 1024
GROUPS_A = 8
GROUP_A = 128
MLP_CHUNK = 128
N_HEADS = 8
HEAD_DIM = 128
CHUNK = 64
LEFT_CHUNKS = 8
KV_WIN = LEFT_CHUNKS * CHUNK
BAND = (LEFT_CHUNKS + 1) * CHUNK
GROUP_CHUNKS = 4
GROUP_Q = GROUP_CHUNKS * CHUNK
GROUP_K = GROUP_Q + KV_WIN
REL_CLIP = 128
D_FF = 5632
D_IN = 2 * D_A + 3 * D_B
EPS = 1e-6
NEG_INF = -1e30
SCALE = HEAD_DIM ** -0.5
LOG2E = 1.4426950408889634

F32 = jnp.float32
BF16 = jnp.bfloat16
LANES = 128

VMEM_LIMIT_BYTES = 60 * 1024 * 1024
FFN_ROW_CHUNK = 256
PROJ_ROW_CHUNK = 256
PROJ_ROW_TILE = 512
MIXER_ROW_TILE = 512
FFN_ROW_TILE = 1024
FFN_HIDDEN_TILE = 512
FFN_HEAD_HIDDEN_TILE = 512
FFN_GATE_COLS = 256
CAST_SLABS = 64


def _rms(x, g):
    return x * lax.rsqrt(jnp.mean(x * x, axis=-1, keepdims=True) + EPS) * g


def _params(sem):
    return pltpu.CompilerParams(dimension_semantics=sem, vmem_limit_bytes=VMEM_LIMIT_BYTES)


def _ffn_rows(x_ref, o_ref, xn_ref, gpre_ref, gpost_ref, wg_ref, wu_ref, wd_ref, *, rc, first, last):
    for r in range(x_ref.shape[0] // rc):
        rows = slice(r * rc, (r + 1) * rc)
        if first:
            xn_ref[rows, :] = _rms(x_ref[rows, :], gpre_ref[...]).astype(BF16)
        xn = xn_ref[rows, :]
        parts = []
        tw = wg_ref.shape[2]
        for t in range(wg_ref.shape[0]):
            for c in range(0, tw, min(tw, FFN_GATE_COLS)):
                cols = slice(c, c + min(tw, FFN_GATE_COLS))
                g = jnp.dot(xn, wg_ref[t, :, cols], preferred_element_type=F32)
                u = jnp.dot(xn, wu_ref[t, :, cols], preferred_element_type=F32)
                parts.append((g * jax.nn.sigmoid(g) * u).astype(wd_ref.dtype))
        a = parts[0] if len(parts) == 1 else jnp.concatenate(parts, axis=1)
        d = jnp.dot(a, wd_ref[...], preferred_element_type=F32)
        if first:
            o_ref[rows, :] = d
        elif last:
            o_ref[rows, :] = x_ref[rows, :] + 0.5 * _rms(o_ref[rows, :] + d, gpost_ref[...])
        else:
            o_ref[rows, :] += d


def _when_first_mid_last(j, nj, enabled, body):
    pl.when(jnp.logical_and(enabled, j == 0))(functools.partial(body, first=True, last=False))
    pl.when(jnp.logical_and(enabled, jnp.logical_and(j > 0, j < nj - 1)))(
        functools.partial(body, first=False, last=False))
    pl.when(jnp.logical_and(enabled, j == nj - 1))(functools.partial(body, first=False, last=True))


def _ffn_head_kernel(xp_ref, xs_ref, gpre_ref, gpost_ref, wgf_ref, wuf_ref, wd_ref,
                     op_ref, os_ref, wg_ref, wu_ref, xnp_ref, xns_ref, *, rc):
    def body(first, last):
        wg_ref[0] = wgf_ref[...].astype(BF16)
        wu_ref[0] = wuf_ref[...].astype(BF16)
        for x_ref, o_ref, xn_ref in ((xp_ref, op_ref, xnp_ref), (xs_ref, os_ref, xns_ref)):
            _ffn_rows(x_ref, o_ref, xn_ref, gpre_ref, gpost_ref, wg_ref, wu_ref, wd_ref,
                      rc=min(rc, x_ref.shape[0]), first=first, last=last)

    _when_first_mid_last(pl.program_id(0), pl.num_programs(0), True, body)


def _ffn_head(xp, xs, gpre, gpost, wg, wu, wd, *, tm, tf):
    nj = D_FF // tf
    ms = xs.shape[0]
    assert nj >= 2
    const = lambda j: (0, 0)
    vec = pl.BlockSpec((1, D_MODEL), const)
    return pl.pallas_call(
        functools.partial(_ffn_head_kernel, rc=FFN_ROW_CHUNK),
        grid=(nj,),
        in_specs=[
            pl.BlockSpec((tm, D_MODEL), const, pipeline_mode=pl.Buffered(1)),
            pl.BlockSpec((ms, D_MODEL), const, pipeline_mode=pl.Buffered(1)),
            vec, vec,
            pl.BlockSpec((D_MODEL, tf), lambda j: (0, j)),
            pl.BlockSpec((D_MODEL, tf), lambda j: (0, j)),
            pl.BlockSpec((tf, D_MODEL), lambda j: (j, 0)),
        ],
        out_specs=[
            pl.BlockSpec((tm, D_MODEL), const),
            pl.BlockSpec((ms, D_MODEL), const),
            pl.BlockSpec((1, D_MODEL, tf), lambda j: (j, 0, 0)),
            pl.BlockSpec((1, D_MODEL, tf), lambda j: (j, 0, 0)),
        ],
        out_shape=[
            jax.ShapeDtypeStruct((tm, D_MODEL), F32),
            jax.ShapeDtypeStruct((ms, D_MODEL), F32),
            jax.ShapeDtypeStruct((nj, D_MODEL, tf), BF16),
            jax.ShapeDtypeStruct((nj, D_MODEL, tf), BF16),
        ],
        scratch_shapes=[pltpu.VMEM((tm, D_MODEL), BF16), pltpu.VMEM((ms, D_MODEL), BF16)],
        compiler_params=_params(("arbitrary",)),
        name="ffn_head",
    )(xp, xs, gpre, gpost, wg, wu, wd)


def _ffn_kernel(x_ref, gpre_ref, gpost_ref, wg_ref, wu_ref, wd_ref, head_ref, *rest, rc, n_cast):
    cast_src, o_ref, cast_dst, xn_ref = rest[:n_cast], rest[n_cast], rest[n_cast + 1:-1], rest[-1]
    i = pl.program_id(0)
    j = pl.program_id(1)

    def body(first, last):
        for src, dst in zip(cast_src, cast_dst, strict=True):
            dst[...] = src[...].astype(BF16)
        _ffn_rows(x_ref, o_ref, xn_ref, gpre_ref, gpost_ref, wg_ref, wu_ref, wd_ref,
                  rc=rc, first=first, last=last)

    _when_first_mid_last(j, pl.num_programs(1), i > 0, body)

    @pl.when(jnp.logical_and(i == 0, j == 0))
    def _():
        pltpu.sync_copy(head_ref, o_ref)


def _ffn(x, gpre, gpost, wg, wu, wd, head, *, tm, tf, cast=()):
    m = x.shape[0]
    sub = tf // wg.shape[2]
    nj = wg.shape[0] // sub
    assert head.shape == (tm, D_MODEL) and m // tm >= 2 and nj >= 2
    assert (m // tm - 1) * nj >= CAST_SLABS
    hid = lambda i, j: jnp.where(i == 0, 0, j)
    slab = lambda i, j: (jnp.clip((i - 1) * nj + j, 0, CAST_SLABS - 1), 0)
    slab_spec = lambda w: pl.BlockSpec((w.shape[0] // CAST_SLABS, w.shape[1]), slab)
    vec = pl.BlockSpec((1, D_MODEL), lambda i, j: (0, 0))
    outs = pl.pallas_call(
        functools.partial(_ffn_kernel, rc=FFN_ROW_CHUNK, n_cast=len(cast)),
        grid=(m // tm, nj),
        in_specs=[
            pl.BlockSpec((tm, D_MODEL), lambda i, j: (jnp.maximum(i, 1), 0)), vec, vec,
            pl.BlockSpec((sub, D_MODEL, tf // sub), lambda i, j: (hid(i, j), 0, 0)),
            pl.BlockSpec((sub, D_MODEL, tf // sub), lambda i, j: (hid(i, j), 0, 0)),
            pl.BlockSpec((tf, D_MODEL), lambda i, j: (hid(i, j), 0)),
            pl.BlockSpec(memory_space=pl.ANY),
        ] + [slab_spec(w) for w in cast],
        out_specs=[pl.BlockSpec((tm, D_MODEL), lambda i, j: (i, 0))] + [slab_spec(w) for w in cast],
        out_shape=[jax.ShapeDtypeStruct((m, D_MODEL), F32)]
        + [jax.ShapeDtypeStruct(w.shape, BF16) for w in cast],
        scratch_shapes=[pltpu.VMEM((tm, D_MODEL), BF16)],
        compiler_params=_params(("arbitrary", "arbitrary")),
        name="ffn",
    )(x, gpre, gpost, wg, wu, wd, head, *cast)
    return outs if cast else outs[0]


def _inproj_rows(h_ref, hn_ref, gpre_ref, w_ref, gva_ref, u_ref, va_ref, q_ref, k_ref, v_ref, tails, rc):
    tm = h_ref.shape[0]
    for r in range(tm // rc):
        rows = slice(r * rc, (r + 1) * rc)
        hn_ref[rows, :] = _rms(h_ref[rows, :], gpre_ref[...]).astype(BF16)
        hn = hn_ref[rows, :]
        z = lambda c: jnp.dot(hn, w_ref[:, c * D_A:(c + 1) * D_A], preferred_element_type=F32)
        u_ref[rows, :] = jax.nn.gelu(z(0))
        va_ref[rows, :] = _rms(jax.nn.gelu(z(1)), gva_ref[...]).astype(va_ref.dtype)
        q_ref[rows, :] = z(2).astype(q_ref.dtype)
        for n, (c, o_ref) in enumerate(((3, k_ref), (4, v_ref))):
            zc = z(c)
            o_ref[rows, :] = zc.astype(o_ref.dtype)
            t0 = r * rc - (tm - KV_WIN)
            if tails is not None and t0 >= 0:
                tails[n][t0:t0 + rc, :] = zc


def _inproj_kernel(hp_ref, hs_ref, gpre_ref, w_ref, gva_ref,
                   up_ref, vap_ref, qp_ref, kp_ref, vp_ref, kt_ref, vt_ref,
                   us_ref, vas_ref, qs_ref, ks_ref, vs_ref, hnp_ref, hns_ref, *, rc):
    _inproj_rows(hp_ref, hnp_ref, gpre_ref, w_ref, gva_ref, up_ref, vap_ref, qp_ref, kp_ref, vp_ref,
                 (kt_ref, vt_ref), rc)

    @pl.when(pl.program_id(0) == 0)
    def _():
        _inproj_rows(hs_ref, hns_ref, gpre_ref, w_ref, gva_ref, us_ref, vas_ref, qs_ref, ks_ref, vs_ref,
                     None, min(rc, hs_ref.shape[0]))


def _inproj(hp, hs, gpre, w_in, gva, *, tm):
    m, ms = hp.shape[0], hs.shape[0]
    rc = min(tm, PROJ_ROW_CHUNK)
    assert tm >= KV_WIN and (tm - KV_WIN) % rc == 0
    row_blk = lambda i: (i, 0)
    const = lambda i: (0, 0)
    p_dtypes = (F32, BF16, BF16, BF16, BF16)
    s_dtypes = (F32, F32, BF16, F32, F32)
    return pl.pallas_call(
        functools.partial(_inproj_kernel, rc=rc),
        grid=(m // tm,),
        in_specs=[
            pl.BlockSpec((tm, D_MODEL), row_blk),
            pl.BlockSpec((ms, D_MODEL), const),
            pl.BlockSpec((1, D_MODEL), const),
            pl.BlockSpec((D_MODEL, D_IN), const, pipeline_mode=pl.Buffered(1)),
            pl.BlockSpec((1, D_A), const),
        ],
        out_specs=[pl.BlockSpec((tm, D_A), row_blk)] * 5 + [pl.BlockSpec((KV_WIN, D_B), const)] * 2
        + [pl.BlockSpec((ms, D_A), const)] * 5,
        out_shape=[jax.ShapeDtypeStruct((m, D_A), d) for d in p_dtypes]
        + [jax.ShapeDtypeStruct((KV_WIN, D_B), F32)] * 2
        + [jax.ShapeDtypeStruct((ms, D_A), d) for d in s_dtypes],
        scratch_shapes=[pltpu.VMEM((tm, D_MODEL), BF16), pltpu.VMEM((ms, D_MODEL), BF16)],
        compiler_params=_params(("arbitrary",)),
        name="inproj",
    )(hp, hs, gpre, w_in, gva)


def _mask_spatial_weights(sw_ref, wm_ref):
    t = lax.broadcasted_iota(jnp.int32, (MLP_CHUNK, MLP_CHUNK), 0)
    s = lax.broadcasted_iota(jnp.int32, (MLP_CHUNK, MLP_CHUNK), 1)
    for g in range(GROUPS_A):
        wm_ref[g] = jnp.where(s <= t, sw_ref[g], 0.0).astype(BF16)


def _attend(qh, kh, vh, bias, valid):
    s = lax.dot_general(qh, kh, (((1,), (1,)), ((), ())), preferred_element_type=F32) * SCALE + bias
    if valid is not None:
        s = jnp.where(valid, s, NEG_INF)
    e = jnp.exp(s - jnp.max(s, axis=-1, keepdims=True))
    p = e / jnp.sum(e, axis=-1, keepdims=True)
    return jnp.dot(p.astype(BF16), vh, preferred_element_type=F32)


def _gmlp_block(u_ref, va_ref, wm_ref, sbb_ref, goa_ref, cat_ref, a_scr):
    nchunk = u_ref.shape[0] // MLP_CHUNK
    chunk_rows = [slice(n * MLP_CHUNK, (n + 1) * MLP_CHUNK) for n in range(nchunk)]
    for g in range(GROUPS_A):
        cols = slice(g * GROUP_A, (g + 1) * GROUP_A)
        v_all = jnp.concatenate([va_ref[rows, cols] for rows in chunk_rows], axis=1)
        mixed = jnp.dot(wm_ref[g], v_all, preferred_element_type=F32)
        for n, rows in enumerate(chunk_rows):
            a_scr[rows, cols] = u_ref[rows, cols] * (mixed[:, n * GROUP_A:(n + 1) * GROUP_A] + sbb_ref[g])
    for rows in chunk_rows:
        cat_ref[rows, 0:D_A] = _rms(a_scr[rows, :], goa_ref[...]).astype(BF16)


def _attn_groups(q_ref, kbuf, vbuf, bias_ref, gob_ref, cat_ref, o_scr, *, first, after_group):
    tb = q_ref.shape[0]
    for gq in range(tb // GROUP_Q):
        q0 = gq * GROUP_Q
        qrows = slice(q0, q0 + GROUP_Q)
        krows = slice(q0, q0 + GROUP_K)
        if first:
            valid = lax.broadcasted_iota(jnp.int32, (GROUP_Q, GROUP_K), 1) >= KV_WIN - q0
        for h in range(N_HEADS):
            cols = slice(h * HEAD_DIM, (h + 1) * HEAD_DIM)
            s2 = lax.dot_general(q_ref[qrows, cols], kbuf[krows, cols], (((1,), (1,)), ((), ())),
                                 preferred_element_type=F32) * (SCALE * LOG2E) + bias_ref[h]
            if first:
                s2 = jnp.where(valid, s2, NEG_INF)
            e = jnp.exp2(s2 - jnp.max(s2, axis=-1, keepdims=True)).astype(BF16)
            nd = jnp.dot(e, vbuf[krows, 2 * h * HEAD_DIM:2 * (h + 1) * HEAD_DIM], preferred_element_type=F32)
            o_scr[:, cols] = nd[:, 0:HEAD_DIM] * (1.0 / nd[:, HEAD_DIM:2 * HEAD_DIM])
        cat_ref[qrows, D_A:D_A + D_B] = _rms(o_scr[...], gob_ref[...]).astype(BF16)
        after_group(qrows)


def _expand_rel_bias(relw_row, lq, lk):
    b = jnp.broadcast_to(relw_row, (lq, relw_row.shape[1]))
    return pltpu.roll(b, 0, 1, stride=1, stride_axis=0)[:, 0:lk]


def _mixer_prompt_kernel(u_ref, va_ref, q_ref, k_ref, v_ref, sw_ref, sbb_ref, relw_ref, goa_ref, gob_ref,
                         h_ref, wout_ref, gpost_ref, o_ref,
                         wm_ref, kbuf, vbuf, a_scr, o_scr, bias_ref, cat_ref):
    i = pl.program_id(0)
    tb = u_ref.shape[0]

    def outproj(rows):
        m = jnp.dot(cat_ref[rows, :], wout_ref[...], preferred_element_type=F32)
        o_ref[rows, :] = h_ref[rows, :] + _rms(m, gpost_ref[...])

    @pl.when(i == 0)
    def _():
        _mask_spatial_weights(sw_ref, wm_ref)
        qc = lax.broadcasted_iota(jnp.int32, (GROUP_Q, GROUP_K), 0) // CHUNK
        jc = lax.broadcasted_iota(jnp.int32, (GROUP_Q, GROUP_K), 1) // CHUNK
        band = jnp.logical_and(jc >= qc, jc <= qc + LEFT_CHUNKS)
        for h in range(N_HEADS):
            bias_ref[h] = jnp.where(band, _expand_rel_bias(relw_ref[h:h + 1, :], GROUP_Q, GROUP_K), NEG_INF)
        kbuf[0:KV_WIN, :] = jnp.zeros((KV_WIN, D_B), BF16)
        for h in range(N_HEADS):
            lo = 2 * h * HEAD_DIM
            vbuf[0:KV_WIN, lo:lo + HEAD_DIM] = jnp.zeros((KV_WIN, HEAD_DIM), BF16)
            vbuf[:, lo + HEAD_DIM:lo + 2 * HEAD_DIM] = jnp.ones((KV_WIN + tb, HEAD_DIM), BF16)

    @pl.when(i > 0)
    def _():
        kbuf[0:KV_WIN, :] = kbuf[tb:tb + KV_WIN, :]
        for h in range(N_HEADS):
            lo = 2 * h * HEAD_DIM
            vbuf[0:KV_WIN, lo:lo + HEAD_DIM] = vbuf[tb:tb + KV_WIN, lo:lo + HEAD_DIM]

    kbuf[KV_WIN:KV_WIN + tb, :] = k_ref[...]
    for h in range(N_HEADS):
        vbuf[KV_WIN:KV_WIN + tb, 2 * h * HEAD_DIM:(2 * h + 1) * HEAD_DIM] = v_ref[:, h * HEAD_DIM:(h + 1) * HEAD_DIM]

    _gmlp_block(u_ref, va_ref, wm_ref, sbb_ref, goa_ref, cat_ref, a_scr)

    @pl.when(i == 0)
    def _():
        _attn_groups(q_ref, kbuf, vbuf, bias_ref, gob_ref, cat_ref, o_scr, first=True, after_group=outproj)

    @pl.when(i > 0)
    def _():
        _attn_groups(q_ref, kbuf, vbuf, bias_ref, gob_ref, cat_ref, o_scr, first=False, after_group=outproj)


def _mixer_prompt(u, va, q, k, v, sw, sbb, relw, goa, gob, h, w_out, gpost, *, tb):
    m = u.shape[0]
    assert tb >= KV_WIN and tb % GROUP_Q == 0
    row_blk = lambda i: (i, 0)
    const2 = lambda i: (0, 0)
    const3 = lambda i: (0, 0, 0)
    return pl.pallas_call(
        _mixer_prompt_kernel,
        grid=(m // tb,),
        in_specs=[
            pl.BlockSpec((tb, D_A), row_blk),
            pl.BlockSpec((tb, D_A), row_blk),
            pl.BlockSpec((tb, D_B), row_blk),
            pl.BlockSpec((tb, D_B), row_blk),
            pl.BlockSpec((tb, D_B), row_blk),
            pl.BlockSpec((GROUPS_A, MLP_CHUNK, MLP_CHUNK), const3),
            pl.BlockSpec((GROUPS_A, MLP_CHUNK, GROUP_A), const3),
            pl.BlockSpec(relw.shape, const2),
            pl.BlockSpec((1, D_A), const2),
            pl.BlockSpec((1, D_B), const2),
            pl.BlockSpec((tb, D_MODEL), row_blk),
            pl.BlockSpec((D_MODEL, D_MODEL), const2, pipeline_mode=pl.Buffered(1)),
            pl.BlockSpec((1, D_MODEL), const2),
        ],
        out_specs=pl.BlockSpec((tb, D_MODEL), row_blk),
        out_shape=jax.ShapeDtypeStruct((m, D_MODEL), F32),
        scratch_shapes=[
            pltpu.VMEM((GROUPS_A, MLP_CHUNK, MLP_CHUNK), BF16),
            pltpu.VMEM((KV_WIN + tb, D_B), BF16),
            pltpu.VMEM((KV_WIN + tb, 2 * D_B), BF16),
            pltpu.VMEM((tb, D_A), F32),
            pltpu.VMEM((GROUP_Q, D_B), F32),
            pltpu.VMEM((N_HEADS, GROUP_Q, GROUP_K), F32),
            pltpu.VMEM((tb, D_MODEL), BF16),
        ],
        compiler_params=_params(("arbitrary",)),
        name="mixer_prompt",
    )(u, va, q, k, v, sw, sbb, relw, goa, gob, h, w_out, gpost)


def _mixer_sample_kernel(u_ref, va_ref, q_ref, k_ref, v_ref, ck_ref, cv_ref, sw_ref, sbb_ref, relw_ref,
                         goa_ref, gob_ref, h_ref, wout_ref, gpost_ref, o_ref,
                         wm_ref, vap, kbuf, vbuf, a_scr, o_scr, bias_ref, cat_all):
    b = pl.program_id(0)
    n = u_ref.shape[0]
    w = ck_ref.shape[1] // N_HEADS
    cat_ref = cat_all.at[pl.ds(pl.multiple_of(b * n, n), n)]

    @pl.when(b == 0)
    def _():
        _mask_spatial_weights(sw_ref, wm_ref)
        for h in range(N_HEADS):
            bias_ref[h] = _expand_rel_bias(relw_ref[h:h + 1, :], n, w + n)
        vap[...] = jnp.zeros(vap.shape, BF16)

    vap[0:n, :] = va_ref[...].astype(BF16)
    for h in range(N_HEADS):
        cols = slice(h * HEAD_DIM, (h + 1) * HEAD_DIM)
        kbuf[0:w, cols] = ck_ref[0, pl.ds(h, w, stride=N_HEADS), :].astype(BF16)
        vbuf[0:w, cols] = cv_ref[0, pl.ds(h, w, stride=N_HEADS), :].astype(BF16)
    kbuf[w:w + n, :] = k_ref[...].astype(BF16)
    vbuf[w:w + n, :] = v_ref[...].astype(BF16)

    ssq = jnp.zeros((n, 1), F32)
    for g in range(GROUPS_A):
        cols = slice(g * GROUP_A, (g + 1) * GROUP_A)
        mixed = jnp.dot(wm_ref[g, 0:n, :], vap[:, cols], preferred_element_type=F32) + sbb_ref[g, 0:n, :]
        a = u_ref[:, cols] * mixed
        a_scr[:, cols] = a
        ssq = ssq + jnp.sum(a * a, axis=-1, keepdims=True)
    inv = lax.rsqrt(ssq * (1.0 / D_A) + EPS)
    cat_ref[:, 0:D_A] = (a_scr[...] * inv * goa_ref[...]).astype(BF16)

    ssq = jnp.zeros((n, 1), F32)
    for h in range(N_HEADS):
        cols = slice(h * HEAD_DIM, (h + 1) * HEAD_DIM)
        o = _attend(q_ref[:, cols], kbuf[:, cols], vbuf[:, cols], bias_ref[h], None)
        o_scr[:, cols] = o
        ssq = ssq + jnp.sum(o * o, axis=-1, keepdims=True)
    inv = lax.rsqrt(ssq * (1.0 / D_B) + EPS)
    cat_ref[:, D_A:D_A + D_B] = (o_scr[...] * inv * gob_ref[...]).astype(BF16)

    @pl.when(b == pl.num_programs(0) - 1)
    def _():
        m = jnp.dot(cat_all[...], wout_ref[...], preferred_element_type=F32)
        o_ref[...] = h_ref[...] + _rms(m, gpost_ref[...])


def _mixer_sample(u, va, q, k, v, ck, cv, sw, sbb, relw, goa, gob, h, w_out, gpost, *, n):
    m = u.shape[0]
    nb, wh, _ = ck.shape
    w = wh // N_HEADS
    row_blk = lambda b: (b, 0)
    const2 = lambda b: (0, 0)
    const3 = lambda b: (0, 0, 0)
    return pl.pallas_call(
        _mixer_sample_kernel,
        grid=(nb,),
        in_specs=[
            pl.BlockSpec((n, D_A), row_blk),
            pl.BlockSpec((n, D_A), row_blk),
            pl.BlockSpec((n, D_B), row_blk),
            pl.BlockSpec((n, D_B), row_blk),
            pl.BlockSpec((n, D_B), row_blk),
            pl.BlockSpec((1, wh, HEAD_DIM), lambda b: (b, 0, 0)),
            pl.BlockSpec((1, wh, HEAD_DIM), lambda b: (b, 0, 0)),
            pl.BlockSpec((GROUPS_A, MLP_CHUNK, MLP_CHUNK), const3),
            pl.BlockSpec((GROUPS_A, MLP_CHUNK, GROUP_A), const3),
            pl.BlockSpec(relw.shape, const2),
            pl.BlockSpec((1, D_A), const2),
            pl.BlockSpec((1, D_B), const2),
            pl.BlockSpec((m, D_MODEL), const2),
            pl.BlockSpec((D_MODEL, D_MODEL), const2, pipeline_mode=pl.Buffered(1)),
            pl.BlockSpec((1, D_MODEL), const2),
        ],
        out_specs=pl.BlockSpec((m, D_MODEL), const2),
        out_shape=jax.ShapeDtypeStruct((m, D_MODEL), F32),
        scratch_shapes=[
            pltpu.VMEM((GROUPS_A, MLP_CHUNK, MLP_CHUNK), BF16),
            pltpu.VMEM((MLP_CHUNK, D_A), BF16),
            pltpu.VMEM((w + n, D_B), BF16),
            pltpu.VMEM((w + n, D_B), BF16),
            pltpu.VMEM((n, D_A), F32),
            pltpu.VMEM((n, D_B), F32),
            pltpu.VMEM((N_HEADS, n, w + n), F32),
            pltpu.VMEM((m, D_MODEL), BF16),
        ],
        compiler_params=_params(("arbitrary",)),
        name="mixer_sample",
    )(u, va, q, k, v, ck, cv, sw, sbb, relw, goa, gob, h, w_out, gpost)


def _rel_bias_rows(rel_bias, lq, lk):
    nh = rel_bias.shape[0]
    p = lq + lk - 1
    t_min, t_max = KV_WIN - lk + 1, KV_WIN + lq - 1
    n_lo = max(0, -REL_CLIP - t_min)
    n_hi = max(0, t_max - REL_CLIP)
    mid = rel_bias[:, max(t_min, -REL_CLIP) + REL_CLIP:min(t_max, REL_CLIP) + REL_CLIP + 1]
    by_dist = jnp.concatenate([jnp.broadcast_to(rel_bias[:, :1], (nh, n_lo)), mid,
                               jnp.broadcast_to(rel_bias[:, -1:], (nh, n_hi))], axis=1).astype(F32)
    assert by_dist.shape == (nh, p)
    padded = jnp.pad(by_dist[:, ::-1], ((0, 0), (0, -p % LANES)))
    return jnp.roll(padded, -(lq - 1), axis=1)


def kernel(x_prompt, x_sample, cache_b_k, cache_b_v, w_in, w_out, sw_a, sb_a, rel_bias, g_va, g_oa, g_ob,
           g_f1_pre, g_f1_post, g_m_pre, g_m_post, g_f2_pre, g_f2_post,
           f1_gate, f1_up, f1_down, f2_gate, f2_up, f2_down):
    depth = w_in.shape[0]
    assert depth == 1
    bp, sp, _ = x_prompt.shape
    bs, ns, _ = x_sample.shape
    assert bp == 1
    l = 0
    row = lambda g: g[l].reshape(1, -1)
    sbb = jnp.broadcast_to(sb_a[l][:, :, None], (GROUPS_A, MLP_CHUNK, GROUP_A))
    bias_p = _rel_bias_rows(rel_bias[l], GROUP_Q, GROUP_K) * LOG2E
    wc = cache_b_k.shape[2]
    bias_s = _rel_bias_rows(rel_bias[l], ns, wc + ns)

    xp = x_prompt.reshape(sp, D_MODEL)
    ms = bs * ns
    xs = x_sample.reshape(ms, D_MODEL)
    ffn_tiles = dict(tm=FFN_ROW_TILE, tf=FFN_HIDDEN_TILE)

    h_head, hs, *f1 = _ffn_head(xp, xs, row(g_f1_pre), row(g_f1_post), f1_gate[l], f1_up[l], f1_down[l],
                                tm=FFN_ROW_TILE, tf=FFN_HEAD_HIDDEN_TILE)
    h, w_in_b, w_out_b = _ffn(xp, row(g_f1_pre), row(g_f1_post), *f1, f1_down[l], h_head, **ffn_tiles,
                              cast=(w_in[l], w_out[l]))

    u, va, q, k, v, kt, vt, us, vas, qs, ks, vs = _inproj(h, hs, row(g_m_pre), w_in_b, row(g_va),
                                                           tm=PROJ_ROW_TILE)
    ck = cache_b_k[l].reshape(bs, wc * N_HEADS, HEAD_DIM)
    cv = cache_b_v[l].reshape(bs, wc * N_HEADS, HEAD_DIM)
    hs2 = _mixer_sample(us, vas, qs, ks, vs, ck, cv, sw_a[l], sbb, bias_s, row(g_oa), row(g_ob),
                        hs, w_out_b, row(g_m_post), n=ns)
    h2 = _mixer_prompt(u, va, q, k, v, sw_a[l], sbb, bias_p, row(g_oa), row(g_ob),
                       h, w_out_b, row(g_m_post), tb=MIXER_ROW_TILE)

    y_head, ys, *f2 = _ffn_head(h2, hs2, row(g_f2_pre), row(g_f2_post), f2_gate[l], f2_up[l], f2_down[l],
                                tm=FFN_ROW_TILE, tf=FFN_HEAD_HIDDEN_TILE)
    yp = _ffn(h2, row(g_f2_pre), row(g_f2_post), *f2, f2_down[l], y_head, **ffn_tiles)

    hd = (N_HEADS, HEAD_DIM)
    return (
        yp.reshape(bp, sp, D_MODEL),
        ys.reshape(bs, ns, D_MODEL),
        kt.reshape(depth, bp, KV_WIN, *hd),
        vt.reshape(depth, bp, KV_WIN, *hd),
        ks.reshape(depth, bs, ns, *hd),
        vs.reshape(depth, bs, ns, *hd),
        vas.reshape(depth, bs, ns, D_A),
    )
```
